```python
import jax, jax.numpy as jnp
from jax import lax
import numpy as np

D_MODEL = 2048
BATCH = 2
SEQ = 16384
DEPTH = 1

GLA_HEADS = 4
GLA_DK = 128
GLA_DV = 256
GLA_WK = GLA_HEADS * GLA_DK
GLA_WV = GLA_HEADS * GLA_DV
GLA_GATE_RANK = 16
GLA_TAU = 16.0
GLA_CHUNK = 64

MOBA_HEADS = 8
MOBA_DH = 128
MOBA_W = MOBA_HEADS * MOBA_DH
MOBA_BLOCK = 256
MOBA_TOPK = 3
MOBA_QCHUNK = 64

NORM_EPS = 1e-6
NEG_BIG = -1e30

SPLIT_SIZES = (GLA_WK, GLA_WK, GLA_WV, GLA_GATE_RANK, GLA_WV,
               MOBA_W, MOBA_W, MOBA_W, MOBA_W, D_MODEL, D_MODEL)
PROJ_WIDTH = sum(SPLIT_SIZES)
SPLIT_POINTS = tuple(int(v) for v in np.cumsum(SPLIT_SIZES)[:-1])

kernel_name = "hybrid_gla_moba_gated_block"


def rms_norm(x, g):
    xf = x.astype(jnp.float32)
    y = xf * lax.rsqrt(jnp.mean(xf * xf, axis=-1, keepdims=True) + NORM_EPS)
    return (y * g.astype(jnp.float32)).astype(x.dtype)


def alibi_slopes(n_heads):
    return jnp.exp2(-8.0 * jnp.arange(1, n_heads + 1, dtype=jnp.float32) / n_heads)


def gla_chunked(q, k, v, log_a):
    B, S, H, dk = q.shape
    dv = v.shape[-1]
    C = GLA_CHUNK
    n = S // C
    f32 = jnp.float32

    def to_chunks(t):
        return t.astype(f32).reshape(B, n, C, H, t.shape[-1]).transpose(1, 0, 3, 2, 4)

    qs, ks, vs, gs = to_chunks(q), to_chunks(k), to_chunks(v), to_chunks(log_a)
    causal = jnp.tril(jnp.ones((C, C), dtype=bool))

    def step(state, inp):
        qc, kc, vc, gc = inp
        b = jnp.cumsum(gc, axis=2)
        o_inter = jnp.einsum('bhcd,bhde->bhce', qc * jnp.exp(b), state)
        diff = b[:, :, :, None, :] - b[:, :, None, :, :]
        decay = jnp.exp(jnp.where(causal[:, :, None], diff, -jnp.inf))
        attn = jnp.sum(qc[:, :, :, None, :] * kc[:, :, None, :, :] * decay, axis=-1)
        o_intra = jnp.einsum('bhij,bhje->bhie', attn, vc)
        b_last = b[:, :, -1:, :]
        state = (jnp.exp(b_last[:, :, 0, :, None]) * state
                 + jnp.einsum('bhcd,bhce->bhde', kc * jnp.exp(b_last - b), vc))
        return state, o_inter + o_intra

    s0 = jnp.zeros((B, H, dk, dv), f32)
    _, o = lax.scan(step, s0, (qs, ks, vs, gs))
    return o.transpose(1, 0, 3, 2, 4).reshape(B, S, H, dv).astype(v.dtype)


def moba_attention(q, k, v):
    B, H, S, dh = q.shape
    f32 = jnp.float32
    S_pad = -(-S // MOBA_BLOCK) * MOBA_BLOCK
    pad = S_pad - S
    padw = ((0, 0), (0, 0), (0, pad), (0, 0))
    q, k, v = jnp.pad(q, padw), jnp.pad(k, padw), jnp.pad(v, padw)
    nb = S_pad // MOBA_BLOCK
    n_sel = min(MOBA_TOPK, nb)
    kb = k.reshape(B, H, nb, MOBA_BLOCK, dh)
    vb = v.reshape(B, H, nb, MOBA_BLOCK, dh)

    k_mean = jnp.mean(kb.astype(f32), axis=3)
    gate = jnp.einsum('bhsd,bhnd->bhsn', q.astype(f32), k_mean)
    q_blk = jnp.arange(S_pad) // MOBA_BLOCK
    past = jnp.arange(nb)[None, :] < q_blk[:, None]
    gate = jnp.where(past[None, None], gate, -jnp.inf)
    _, sel = lax.top_k(gate, n_sel)

    QC = MOBA_QCHUNK
    n_qc = S_pad // QC
    q_c = q.reshape(B, H, n_qc, QC, dh).transpose(2, 0, 1, 3, 4)
    sel_c = sel.reshape(B, H, n_qc, QC, n_sel).transpose(2, 0, 1, 3, 4)
    slopes = alibi_slopes(H)
    scale = dh ** -0.5
    bi = jnp.arange(B)[:, None, None, None]
    hi = jnp.arange(H)[None, :, None, None]
    offs = jnp.arange(MOBA_BLOCK)

    def attend(inp):
        ci, qc, selc = inp
        t = ci * QC + jnp.arange(QC)
        own = (ci * QC) // MOBA_BLOCK
        k_own = lax.dynamic_index_in_dim(kb, own, axis=2, keepdims=False)
        v_own = lax.dynamic_index_in_dim(vb, own, axis=2, keepdims=False)
        k_g = kb[bi, hi, selc]
        v_g = vb[bi, hi, selc]
        s_sel = jnp.einsum('bhqd,bhqnjd->bhqnj', qc, k_g,
                           preferred_element_type=f32) * scale
        s_own = jnp.einsum('bhqd,bhjd->bhqj', qc, k_own,
                           preferred_element_type=f32) * scale
        pos_sel = selc[..., None] * MOBA_BLOCK + offs
        pos_own = own * MOBA_BLOCK + offs
        dist_sel = (t[None, None, :, None, None] - pos_sel).astype(f32)
        dist_own = (t[:, None] - pos_own[None, :]).astype(f32)
        s_sel = s_sel - slopes[None, :, None, None, None] * dist_sel
        s_own = s_own - slopes[None, :, None, None] * dist_own
        valid_sel = jnp.arange(n_sel)[None, :] < (t // MOBA_BLOCK)[:, None]
        s_sel = jnp.where(valid_sel[None, None, :, :, None], s_sel, NEG_BIG)
        s_own = jnp.where((pos_own[None, :] <= t[:, None])[None, None], s_own, NEG_BIG)
        scores = jnp.concatenate([s_sel.reshape(B, H, QC, n_sel * MOBA_BLOCK), s_own], axis=-1)
        p = jax.nn.softmax(scores, axis=-1)
        p_sel = p[..., :n_sel * MOBA_BLOCK].reshape(B, H, QC, n_sel, MOBA_BLOCK).astype(v.dtype)
        p_own = p[..., n_sel * MOBA_BLOCK:].astype(v.dtype)
        out = (jnp.einsum('bhqnj,bhqnjd->bhqd', p_sel, v_g, preferred_element_type=f32)
               + jnp.einsum('bhqj,bhjd->bhqd', p_own, v_own, preferred_element_type=f32))
        return out.astype(v.dtype)

    o = lax.map(attend, (jnp.arange(n_qc), q_c, sel_c))
    o = o.transpose(1, 2, 0, 3, 4).reshape(B, H, S_pad, dh)
    return o[:, :, :S]


def setup_inputs(seed: int = 0) -> dict:
    key = jax.random.key(seed)
    ks = jax.random.split(key, 12)
    nrm = jax.random.normal
    f32 = jnp.float32
    x = nrm(ks[0], (BATCH, SEQ, D_MODEL), f32)
    norm_g = 1.0 + 0.02 * nrm(ks[1], (DEPTH, D_MODEL), f32)
    w_in = nrm(ks[2], (DEPTH, D_MODEL, PROJ_WIDTH), f32) * D_MODEL ** -0.5
    w_gla_gate = nrm(ks[3], (DEPTH, GLA_GATE_RANK, GLA_WK), f32) * GLA_GATE_RANK ** -0.5
    b_gla_gate = 0.1 * nrm(ks[4], (DEPTH, GLA_WK), f32)
    gla_out_g = 1.0 + 0.02 * nrm(ks[5], (DEPTH, GLA_DV), f32)
    q_norm_g = 1.0 + 0.02 * nrm(ks[6], (DEPTH, MOBA_DH), f32)
    k_norm_g = 1.0 + 0.02 * nrm(ks[7], (DEPTH, MOBA_DH), f32)
    w_branch_gla = nrm(ks[8], (DEPTH, GLA_WV, D_MODEL), f32) * GLA_WV ** -0.5
    w_branch_moba = nrm(ks[9], (DEPTH, MOBA_W, D_MODEL), f32) * MOBA_W ** -0.5
    w_out = nrm(ks[10], (DEPTH, D_MODEL, D_MODEL), f32) * D_MODEL ** -0.5
    return {"x": x, "norm_g": norm_g, "w_in": w_in, "w_gla_gate": w_gla_gate,
            "b_gla_gate": b_gla_gate, "gla_out_g": gla_out_g, "q_norm_g": q_norm_g,
            "k_norm_g": k_norm_g, "w_branch_gla": w_branch_gla,
            "w_branch_moba": w_branch_moba, "w_out": w_out}


def reference(x, norm_g, w_in, w_gla_gate, b_gla_gate, gla_out_g, q_norm_g, k_norm_g,
              w_branch_gla, w_branch_moba, w_out):
    B, S, _ = x.shape
    for layer in range(DEPTH):
        h = rms_norm(x, norm_g[layer])
        proj = h @ w_in[layer]
        (g_q, g_k, g_v, g_lr, g_silu, m_q, m_k, m_v, m_silu,
         gate_a, gate_b) = jnp.split(proj, SPLIT_POINTS, axis=-1)

        q = g_q.reshape(B, S, GLA_HEADS, GLA_DK) * (GLA_DK ** -0.5)
        k = g_k.reshape(B, S, GLA_HEADS, GLA_DK)
        v = g_v.reshape(B, S, GLA_HEADS, GLA_DV)
        log_a = jax.nn.log_sigmoid(
            (g_lr @ w_gla_gate[layer] + b_gla_gate[layer]).astype(jnp.float32)) / GLA_TAU
        log_a = log_a.reshape(B, S, GLA_HEADS, GLA_DK)
        o_gla = gla_chunked(q, k, v, log_a)
        o_gla = rms_norm(o_gla, gla_out_g[layer]).reshape(B, S, GLA_WV) * jax.nn.silu(g_silu)
        z_gla = o_gla @ w_branch_gla[layer]

        mq = rms_norm(m_q.reshape(B, S, MOBA_HEADS, MOBA_DH), q_norm_g[layer]).transpose(0, 2, 1, 3)
        mk = rms_norm(m_k.reshape(B, S, MOBA_HEADS, MOBA_DH), k_norm_g[layer]).transpose(0, 2, 1, 3)
        mv = m_v.reshape(B, S, MOBA_HEADS, MOBA_DH).transpose(0, 2, 1, 3)
        o_moba = moba_attention(mq, mk, mv)
        o_moba = o_moba.transpose(0, 2, 1, 3).reshape(B, S, MOBA_W) * jax.nn.silu(m_silu)
        z_moba = o_moba @ w_branch_moba[layer]

        merged = jax.nn.sigmoid(gate_a) * z_gla + jax.nn.sigmoid(gate_b) * z_moba
        x = x + (merged @ w_out[layer]).astype(x.dtype)
    return x
```

```python
import functools

import jax
import jax.numpy as jnp
from jax import lax
from jax.experimental import pallas as pl
from jax.experimental.pallas import tpu as pltpu

F32 = jnp.float32
BF16 = jnp.bfloat16
HIGHEST = lax.Precision.HIGHEST

GLA_HEADS = 4
GLA_DK = 128
GLA_DV = 256
GLA_WK = GLA_HEADS * GLA_DK
GLA_WV = GLA_HEADS * GLA_DV
GLA_GATE_RANK = 16
GLA_TAU = 16.0
GLA_CHUNK = 128

MOBA_HEADS = 8
MOBA_DH = 128
MOBA_W = MOBA_HEADS * MOBA_DH
MOBA_BLOCK = 256
MOBA_TOPK = 3

NORM_EPS = 1e-6
NEG_BIG = -1e30
LANES = 128
VMEM_LIMIT_BYTES = 48 * 1024 * 1024


def _params(*semantics):
    return pltpu.CompilerParams(dimension_semantics=semantics,
                                vmem_limit_bytes=VMEM_LIMIT_BYTES)


def _norm_kernel(x_ref, g_ref, h_ref, ht_ref):
    x = x_ref[...]
    ms = jnp.mean(x * x, axis=-1, keepdims=True)
    y = x * lax.rsqrt(ms + NORM_EPS) * g_ref[...]
    h_ref[...] = y.astype(BF16)
    ht_ref[...] = y.T.astype(BF16)


def _rms_norm_both(x, g, tm=512):
    B, S, D = x.shape
    return pl.pallas_call(
        _norm_kernel,
        grid=(B, S // tm),
        in_specs=[pl.BlockSpec((None, tm, D), lambda b, s: (b, s, 0)),
                  pl.BlockSpec((1, D), lambda b, s: (0, 0))],
        out_specs=[pl.BlockSpec((None, tm, D), lambda b, s: (b, s, 0)),
                   pl.BlockSpec((None, D, tm), lambda b, s: (b, 0, s))],
        out_shape=[jax.ShapeDtypeStruct((B, S, D), BF16),
                   jax.ShapeDtypeStruct((B, D, S), BF16)],
        compiler_params=_params("parallel", "parallel"),
        name="rms_norm",
    )(x, g.reshape(1, D))


def _silu(a):
    return a * jax.nn.sigmoid(a)


def _proj_kernel(h_ref, w_ref, o_ref, *, epilogue):
    acc = jnp.dot(h_ref[...], w_ref[...], preferred_element_type=F32)
    o_ref[...] = epilogue(acc).astype(o_ref.dtype)


def _proj(h, w, epilogue, tm=512, tn=2048):
    T, D = h.shape
    N = w.shape[1]
    tn = min(tn, N)
    return pl.pallas_call(
        functools.partial(_proj_kernel, epilogue=epilogue),
        grid=(N // tn, T // tm),
        in_specs=[pl.BlockSpec((tm, D), lambda n, t: (t, 0)),
                  pl.BlockSpec((D, tn), lambda n, t: (0, n))],
        out_specs=pl.BlockSpec((tm, tn), lambda n, t: (t, n)),
        out_shape=jax.ShapeDtypeStruct((T, N), BF16),
        compiler_params=_params("parallel", "parallel"),
        name="proj",
    )(h, w)


def _proj_headnorm_kernel(h_ref, w_ref, g_ref, o_ref):
    acc = jnp.dot(h_ref[...], w_ref[...], preferred_element_type=F32)
    g = g_ref[...]
    for hd in range(acc.shape[1] // MOBA_DH):
        cols = slice(hd * MOBA_DH, (hd + 1) * MOBA_DH)
        a = acc[:, cols]
        ms = jnp.mean(a * a, axis=-1, keepdims=True)
        o_ref[:, cols] = (a * lax.rsqrt(ms + NORM_EPS) * g).astype(o_ref.dtype)


def _proj_headnorm(h, w, g, tm=512):
    T, D = h.shape
    N = w.shape[1]
    return pl.pallas_call(
        _proj_headnorm_kernel,
        grid=(T // tm,),
        in_specs=[pl.BlockSpec((tm, D), lambda t: (t, 0)),
                  pl.BlockSpec((D, N), lambda t: (0, 0)),
                  pl.BlockSpec((1, MOBA_DH), lambda t: (0, 0))],
        out_specs=pl.BlockSpec((tm, N), lambda t: (t, 0)),
        out_shape=jax.ShapeDtypeStruct((T, N), BF16),
        compiler_params=_params("parallel"),
        name="proj_knorm",
    )(h, w, g.reshape(1, MOBA_DH))


def _projt_kernel(wt_ref, ht_ref, g_ref, o_ref, *, norm, scale):
    acc = jnp.dot(wt_ref[...], ht_ref[...], preferred_element_type=F32)
    tn, tm = acc.shape
    if norm:
        a = acc.reshape(tn // MOBA_DH, MOBA_DH, tm)
        ms = jnp.mean(a * a, axis=1, keepdims=True)
        a = a * lax.rsqrt(ms + NORM_EPS) * (g_ref[...] * scale)[None]
        acc = a.reshape(tn, tm)
    for i in range(tm // MOBA_BLOCK):
        o_ref[i] = acc[:, i * MOBA_BLOCK:(i + 1) * MOBA_BLOCK].astype(o_ref.dtype)


def _proj_t(wt, ht, g, norm, scale=1.0, tm=512):
    N, D = wt.shape
    B, _, S = ht.shape
    return pl.pallas_call(
        functools.partial(_projt_kernel, norm=norm, scale=scale),
        grid=(B, S // tm),
        in_specs=[pl.BlockSpec((N, D), lambda b, s: (0, 0)),
                  pl.BlockSpec((None, D, tm), lambda b, s: (b, 0, s)),
                  pl.BlockSpec((MOBA_DH, 1), lambda b, s: (0, 0))],
        out_specs=pl.BlockSpec((None, tm // MOBA_BLOCK, N, MOBA_BLOCK), lambda b, s: (b, s, 0, 0)),
        out_shape=jax.ShapeDtypeStruct((B, S // MOBA_BLOCK, N, MOBA_BLOCK), BF16),
        compiler_params=_params("parallel", "parallel"),
        name="proj_t",
    )(wt, ht, g.reshape(MOBA_DH, 1))


def _loga_kernel(h_ref, wlr_ref, wg_ref, b_ref, o_ref):
    lr = jnp.dot(h_ref[...], wlr_ref[...], preferred_element_type=F32)
    z = jnp.dot(lr, wg_ref[...], precision=HIGHEST, preferred_element_type=F32) + b_ref[...]
    log_sig = jnp.minimum(z, 0.0) - jnp.log1p(jnp.exp(-jnp.abs(z)))
    o_ref[...] = log_sig * (1.0 / GLA_TAU)


def _log_decay(h, w_lr, w_gate, b_gate, tm=512):
    T, D = h.shape
    w_lr = jnp.pad(w_lr, ((0, 0), (0, LANES - GLA_GATE_RANK)))
    w_gate = jnp.pad(w_gate, ((0, LANES - GLA_GATE_RANK), (0, 0)))
    return pl.pallas_call(
        _loga_kernel,
        grid=(T // tm,),
        in_specs=[pl.BlockSpec((tm, D), lambda t: (t, 0)),
                  pl.BlockSpec((D, LANES), lambda t: (0, 0)),
                  pl.BlockSpec((LANES, GLA_WK), lambda t: (0, 0)),
                  pl.BlockSpec((1, GLA_WK), lambda t: (0, 0))],
        out_specs=pl.BlockSpec((tm, GLA_WK), lambda t: (t, 0)),
        out_shape=jax.ShapeDtypeStruct((T, GLA_WK), F32),
        compiler_params=_params("parallel"),
        name="log_decay",
    )(h, w_lr, w_gate, b_gate.reshape(1, GLA_WK))


def _gla_kernel(q_ref, k_ref, v_ref, la_ref, sg_ref, g_ref, o_ref, st_ref, *, chunks):
    @pl.when(pl.program_id(2) == 0)
    def _():
        st_ref[...] = jnp.zeros_like(st_ref)

    C = GLA_CHUNK
    row = lax.broadcasted_iota(jnp.int32, (C, C), 0)
    col = lax.broadcasted_iota(jnp.int32, (C, C), 1)
    causal = col <= row
    tril = causal.astype(F32)
    scale = GLA_DK ** -0.5
    for c in range(chunks):
        rows = slice(c * C, (c + 1) * C)
        b = jnp.dot(tril, la_ref[rows, :], precision=HIGHEST, preferred_element_type=F32)
        b_last = b[C - 1:C, :]
        q = q_ref[rows, :].astype(F32) * scale
        k = k_ref[rows, :].astype(F32)
        v = v_ref[rows, :]
        q_dec = (q * jnp.exp(b)).astype(BF16)
        k_inv_t = (k * jnp.exp(-b)).T.astype(BF16)
        k_rem_t = (k * jnp.exp(b_last - b)).T.astype(BF16)
        dec_t = jnp.broadcast_to(jnp.exp(b_last), (C, GLA_DK)).T
        attn = jnp.dot(q_dec, k_inv_t, preferred_element_type=F32)
        attn = jnp.where(causal, attn, 0.0).astype(BF16)
        st = st_ref[...]
        o = (jnp.dot(q_dec, st.astype(BF16), preferred_element_type=F32)
             + jnp.dot(attn, v, preferred_element_type=F32))
        st_ref[...] = (st * jnp.concatenate([dec_t] * (GLA_DV // C), axis=1)
                       + jnp.dot(k_rem_t, v, preferred_element_type=F32))
        ms = jnp.mean(o * o, axis=-1, keepdims=True)
        y = o * lax.rsqrt(ms + NORM_EPS) * g_ref[...]
        o_ref[rows, :] = (y * sg_ref[rows, :].astype(F32)).astype(o_ref.dtype)


def _gla(qkv, log_a, silu, out_g, B, S, tb=512):
    T = B * S
    nt = S // tb
    kq = GLA_WK // GLA_DK
    kv = 2 * GLA_WK // GLA_DV
    tok = lambda b, h, i: b * nt + i
    return pl.pallas_call(
        functools.partial(_gla_kernel, chunks=tb // GLA_CHUNK),
        grid=(B, GLA_HEADS, nt),
        in_specs=[pl.BlockSpec((tb, GLA_DK), lambda b, h, i: (tok(b, h, i), h)),
                  pl.BlockSpec((tb, GLA_DK), lambda b, h, i: (tok(b, h, i), kq + h)),
                  pl.BlockSpec((tb, GLA_DV), lambda b, h, i: (tok(b, h, i), kv + h)),
                  pl.BlockSpec((tb, GLA_DK), lambda b, h, i: (tok(b, h, i), h)),
                  pl.BlockSpec((tb, GLA_DV), lambda b, h, i: (tok(b, h, i), h)),
                  pl.BlockSpec((1, GLA_DV), lambda b, h, i: (0, 0))],
        out_specs=pl.BlockSpec((tb, GLA_DV), lambda b, h, i: (tok(b, h, i), h)),
        out_shape=jax.ShapeDtypeStruct((T, GLA_WV), BF16),
        scratch_shapes=[pltpu.VMEM((GLA_DK, GLA_DV), F32)],
        compiler_params=_params("parallel", "parallel", "arbitrary"),
        name="gla",
    )(qkv, qkv, qkv, log_a, silu, out_g.reshape(1, GLA_DV))


def _moba_kernel(slope_ref, qt_ref, k_ref, vt_ref, sm_ref, o_ref,
                 kmean_ref, selb_ref, m_ref, l_ref, acc_ref):
    hd = pl.program_id(1)
    qb = pl.program_id(2)
    blk = MOBA_BLOCK
    nb = kmean_ref.shape[0]
    slope = slope_ref[hd]

    @pl.when(qb == 0)
    def _():
        kmean_ref[...] = jnp.zeros_like(kmean_ref)

    q_t = qt_ref[...]

    jidx = lax.broadcasted_iota(jnp.int32, (nb, blk), 0)
    jf = jidx.astype(F32)
    past = jidx < qb
    gate = jnp.dot(kmean_ref[...], q_t.astype(F32), precision=HIGHEST, preferred_element_type=F32)
    gate = jnp.where(past, gate, -jnp.inf)
    sel = jnp.zeros((nb, blk), F32)
    for _ in range(MOBA_TOPK):
        best = jnp.max(gate, axis=0, keepdims=True)
        first = jnp.min(jnp.where(gate == best, jf, float(nb)), axis=0, keepdims=True)
        pick = jf == first
        sel = jnp.where(pick, 1.0, sel)
        gate = jnp.where(pick, -jnp.inf, gate)
    selb_ref[...] = jnp.where((sel > 0.0) & past, 0.0, NEG_BIG)

    key = lax.broadcasted_iota(jnp.int32, (blk, blk), 0)
    qry = lax.broadcasted_iota(jnp.int32, (blk, blk), 1)
    key_bias = key.astype(F32) * slope
    k_own = k_ref[pl.ds(pl.multiple_of(qb * blk, blk), blk), :]
    kmean_ref[pl.ds(qb, 1), :] = jnp.mean(k_own.astype(F32), axis=0, keepdims=True)
    s = jnp.dot(k_own, q_t, preferred_element_type=F32) + key_bias
    s = jnp.where(key <= qry, s, NEG_BIG)
    m0 = jnp.max(s, axis=0, keepdims=True)
    p = jnp.exp(s - m0)
    m_ref[...] = m0
    l_ref[...] = jnp.sum(p, axis=0, keepdims=True)
    acc_ref[...] = jnp.dot(vt_ref[qb], p.astype(BF16), preferred_element_type=F32)

    def past_block(j, carry):
        k_j = k_ref[pl.ds(pl.multiple_of(j * blk, blk), blk), :]
        q_bias = selb_ref[pl.ds(j, 1), :] - slope * ((qb - j).astype(F32) * blk)
        s = jnp.dot(k_j, q_t, preferred_element_type=F32) + key_bias + q_bias
        m_old = m_ref[...]
        m_new = jnp.maximum(m_old, jnp.max(s, axis=0, keepdims=True))
        alpha = jnp.exp(m_old - m_new)
        p = jnp.exp(s - m_new)
        l_ref[...] = alpha * l_ref[...] + jnp.sum(p, axis=0, keepdims=True)
        acc_ref[...] = alpha * acc_ref[...] + jnp.dot(vt_ref[j], p.astype(BF16),
                                                      preferred_element_type=F32)
        m_ref[...] = m_new
        return carry

    lax.fori_loop(0, qb, past_block, 0)

    o_t = acc_ref[...] / l_ref[...]
    o_ref[...] = (o_t.T * sm_ref[...].astype(F32)).astype(o_ref.dtype)


def _moba(q_t, k, v_t, silu, silu_col0, slopes):
    B, nb, W, blk = q_t.shape
    S = nb * blk
    dh = MOBA_DH
    c0 = silu_col0 // dh
    return pl.pallas_call(
        _moba_kernel,
        grid=(B, MOBA_HEADS, nb),
        in_specs=[pl.BlockSpec(memory_space=pltpu.SMEM),
                  pl.BlockSpec((None, None, dh, blk), lambda b, h, i: (b, i, h, 0)),
                  pl.BlockSpec((None, S, dh), lambda b, h, i: (b, 0, h)),
                  pl.BlockSpec((None, nb, dh, blk), lambda b, h, i: (b, 0, h, 0)),
                  pl.BlockSpec((None, blk, dh), lambda b, h, i: (b, i, c0 + h))],
        out_specs=pl.BlockSpec((None, blk, dh), lambda b, h, i: (b, i, h)),
        out_shape=jax.ShapeDtypeStruct((B, S, W), BF16),
        scratch_shapes=[pltpu.VMEM((nb, dh), F32),
                        pltpu.VMEM((nb, blk), F32),
                        pltpu.VMEM((1, blk), F32),
                        pltpu.VMEM((1, blk), F32),
                        pltpu.VMEM((dh, blk), F32)],
        compiler_params=_params("parallel", "parallel", "arbitrary"),
        name="moba",
    )(slopes, q_t, k, v_t, silu)


def _tail_kernel(ya_ref, yb_ref, ga_ref, gb_ref, x_ref, wa_ref, wb_ref, wo_ref, o_ref):
    za = jnp.dot(ya_ref[...], wa_ref[...], preferred_element_type=F32)
    zb = jnp.dot(yb_ref[...], wb_ref[...], preferred_element_type=F32)
    merged = ga_ref[...].astype(F32) * za + gb_ref[...].astype(F32) * zb
    o_ref[...] = x_ref[...] + jnp.dot(merged.astype(BF16), wo_ref[...], preferred_element_type=F32)


def _tail(ya, yb, gates, x, wa, wb, wo, tm=256):
    T, D = x.shape
    const = lambda t: (0, 0)
    return pl.pallas_call(
        _tail_kernel,
        grid=(T // tm,),
        in_specs=[pl.BlockSpec((tm, GLA_WV), lambda t: (t, 0)),
                  pl.BlockSpec((tm, MOBA_W), lambda t: (t, 0)),
                  pl.BlockSpec((tm, D), lambda t: (t, 0)),
                  pl.BlockSpec((tm, D), lambda t: (t, 1)),
                  pl.BlockSpec((tm, D), lambda t: (t, 0)),
                  pl.BlockSpec((GLA_WV, D), const),
                  pl.BlockSpec((MOBA_W, D), const),
                  pl.BlockSpec((D, D), const)],
        out_specs=pl.BlockSpec((tm, D), lambda t: (t, 0)),
        out_shape=jax.ShapeDtypeStruct((T, D), F32),
        compiler_params=_params("parallel"),
        name="merge_out",
    )(ya, yb, gates, gates, x, wa, wb, wo)


def kernel(x, norm_g, w_in, w_gla_gate, b_gla_gate, gla_out_g, q_norm_g, k_norm_g,
           w_branch_gla, w_branch_moba, w_out):
    B, S, D = x.shape
    T = B * S
    sizes = (GLA_WK, GLA_WK, GLA_WV, GLA_GATE_RANK, GLA_WV, MOBA_W, MOBA_W, MOBA_W, MOBA_W, D, D)
    offs = [0]
    for n in sizes:
        offs.append(offs[-1] + n)
    slopes = jnp.exp2(-8.0 * jnp.arange(1, MOBA_HEADS + 1, dtype=F32) / MOBA_HEADS)

    for layer in range(norm_g.shape[0]):
        w = w_in[layer]
        cols = lambda i, j=None: w[:, offs[i]:offs[(i if j is None else j) + 1]]
        w_qkv = cols(0, 2).astype(BF16)
        w_lr = cols(3).astype(BF16)
        w_silu = jnp.concatenate([cols(4), cols(8)], axis=1).astype(BF16)
        w_mq_t = cols(5).T.astype(BF16)
        w_mk = cols(6).astype(BF16)
        w_mv_t = cols(7).T.astype(BF16)
        w_gates = cols(9, 10).astype(BF16)

        h, h_t = _rms_norm_both(x, norm_g[layer])
        h = h.reshape(T, D)

        qkv = _proj(h, w_qkv, lambda a: a)
        log_a = _log_decay(h, w_lr, w_gla_gate[layer], b_gla_gate[layer])
        silu = _proj(h, w_silu, _silu)
        gates = _proj(h, w_gates, jax.nn.sigmoid)
        m_k = _proj_headnorm(h, w_mk, k_norm_g[layer])
        m_q_t = _proj_t(w_mq_t, h_t, q_norm_g[layer], norm=True, scale=MOBA_DH ** -0.5)
        m_v_t = _proj_t(w_mv_t, h_t, q_norm_g[layer], norm=False)

        ya = _gla(qkv, log_a, silu, gla_out_g[layer], B, S)
        yb = _moba(m_q_t, m_k.reshape(B, S, MOBA_W), m_v_t, silu.reshape(B, S, 2 * GLA_WV),
                   GLA_WV, slopes)

        x = _tail(ya, yb.reshape(T, MOBA_W), gates, x.reshape(T, D),
                  w_branch_gla[layer].astype(BF16), w_branch_moba[layer].astype(BF16),
                  w_out[layer].astype(BF16)).reshape(B, S, D)
    return x
```

```python
import functools

import jax
import jax.numpy as jnp
from jax import lax
from jax.experimental import pallas as pl
from jax.experimental.pallas import tpu as pltpu

F32 = jnp.float32
BF16 = jnp.bfloat16
HIGHEST = lax.Precision.HIGHEST

GLA_HEADS = 4
GLA_DK = 128
GLA_DV = 256
GLA_WK = GLA_HEADS * GLA_DK
GLA_WV = GLA_HEADS * GLA_DV
GLA_GATE_RANK = 16
GLA_TAU = 16.0
GLA_CHUNK = 128

MOBA_HEADS = 8
MOBA_DH = 128
MOBA_W = MOBA_HEADS * MOBA_DH
MOBA_BLOCK = 256
MOBA_TOPK = 3
MOBA_KHAT = 2 * MOBA_DH
MOBA_ROUTE_ROWS = 64
MOBA_GROUP = 4

NORM_EPS = 1e-6
NEG_BIG = -1e30
LANES = 128
VMEM_LIMIT_BYTES = 48 * 1024 * 1024


def _params(*semantics):
    return pltpu.CompilerParams(dimension_semantics=semantics,
                                vmem_limit_bytes=VMEM_LIMIT_BYTES)


def _norm_kernel(x_ref, g_ref, h_ref, ht_ref):
    x = x_ref[...]
    ms = jnp.mean(x * x, axis=-1, keepdims=True)
    y = x * lax.rsqrt(ms + NORM_EPS) * g_ref[...]
    h_ref[...] = y.astype(BF16)
    ht_ref[...] = y.T.astype(BF16)


def _rms_norm_both(x, g, tm=512):
    B, S, D = x.shape
    return pl.pallas_call(
        _norm_kernel,
        grid=(B, S // tm),
        in_specs=[pl.BlockSpec((None, tm, D), lambda b, s: (b, s, 0)),
                  pl.BlockSpec((1, D), lambda b, s: (0, 0))],
        out_specs=[pl.BlockSpec((None, tm, D), lambda b, s: (b, s, 0)),
                   pl.BlockSpec((None, D, tm), lambda b, s: (b, 0, s))],
        out_shape=[jax.ShapeDtypeStruct((B, S, D), BF16),
                   jax.ShapeDtypeStruct((B, D, S), BF16)],
        compiler_params=_params("parallel", "parallel"),
        name="rms_norm",
    )(x, g.reshape(1, D))


def _silu(a):
    return a * jax.nn.sigmoid(a)


def _proj_kernel(h_ref, w_ref, o_ref, *, epilogue):
    acc = jnp.dot(h_ref[...], w_ref[...], preferred_element_type=F32)
    o_ref[...] = epilogue(acc).astype(o_ref.dtype)


def _proj(h, w, epilogue, tm=512, tn=2048):
    T, D = h.shape
    N = w.shape[1]
    tn = min(tn, N)
    return pl.pallas_call(
        functools.partial(_proj_kernel, epilogue=epilogue),
        grid=(N // tn, T // tm),
        in_specs=[pl.BlockSpec((tm, D), lambda n, t: (t, 0)),
                  pl.BlockSpec((D, tn), lambda n, t: (0, n))],
        out_specs=pl.BlockSpec((tm, tn), lambda n, t: (t, n)),
        out_shape=jax.ShapeDtypeStruct((T, N), BF16),
        compiler_params=_params("parallel", "parallel"),
        name="proj",
    )(h, w)


def _proj_khat_kernel(h_ref, w_ref, g_ref, e_ref, o_ref):
    acc = jnp.dot(h_ref[...], w_ref[...], preferred_element_type=F32)
    g = g_ref[...]
    for hd in range(acc.shape[1] // MOBA_DH):
        a = acc[:, hd * MOBA_DH:(hd + 1) * MOBA_DH]
        ms = jnp.mean(a * a, axis=-1, keepdims=True)
        base = hd * MOBA_KHAT
        o_ref[:, base:base + MOBA_DH] = (a * lax.rsqrt(ms + NORM_EPS) * g).astype(o_ref.dtype)
        o_ref[:, base + MOBA_DH:base + MOBA_KHAT] = e_ref[...]


def _proj_khat(h, w, g, extras, tm=512):
    T, D = h.shape
    N = w.shape[1]
    ns = extras.shape[0] // tm
    return pl.pallas_call(
        _proj_khat_kernel,
        grid=(T // tm,),
        in_specs=[pl.BlockSpec((tm, D), lambda t: (t, 0)),
                  pl.BlockSpec((D, N), lambda t: (0, 0)),
                  pl.BlockSpec((1, MOBA_DH), lambda t: (0, 0)),
                  pl.BlockSpec((tm, MOBA_DH), lambda t: (t % ns, 0))],
        out_specs=pl.BlockSpec((tm, MOBA_HEADS * MOBA_KHAT), lambda t: (t, 0)),
        out_shape=jax.ShapeDtypeStruct((T, MOBA_HEADS * MOBA_KHAT), BF16),
        compiler_params=_params("parallel"),
        name="proj_khat",
    )(h, w, g.reshape(1, MOBA_DH), extras)


def _projt_kernel(wt_ref, ht_ref, g_ref, *o_refs, norm, scale):
    acc = jnp.dot(wt_ref[...], ht_ref[...], preferred_element_type=F32)
    tn, tm = acc.shape
    if norm:
        a = acc.reshape(tn // MOBA_DH, MOBA_DH, tm)
        ms = jnp.mean(a * a, axis=1, keepdims=True)
        a = a * lax.rsqrt(ms + NORM_EPS) * (g_ref[...] * scale)[None]
        acc = a.reshape(tn, tm)
    for o_ref in o_refs:
        blk = o_ref.shape[-1]
        for i in range(tm // blk):
            o_ref[i] = acc[:, i * blk:(i + 1) * blk].astype(o_ref.dtype)


def _proj_t(wt, ht, g, norm, blks, scale=1.0, tm=1024):
    N, D = wt.shape
    B, _, S = ht.shape
    tm = min(tm, S)
    return pl.pallas_call(
        functools.partial(_projt_kernel, norm=norm, scale=scale),
        grid=(B, S // tm),
        in_specs=[pl.BlockSpec((N, D), lambda b, s: (0, 0)),
                  pl.BlockSpec((None, D, tm), lambda b, s: (b, 0, s)),
                  pl.BlockSpec((MOBA_DH, 1), lambda b, s: (0, 0))],
        out_specs=[pl.BlockSpec((None, tm // blk, N, blk), lambda b, s: (b, s, 0, 0)) for blk in blks],
        out_shape=[jax.ShapeDtypeStruct((B, S // blk, N, blk), BF16) for blk in blks],
        compiler_params=_params("parallel", "parallel"),
        name="proj_t",
    )(wt, ht, g.reshape(MOBA_DH, 1))


def _loga_kernel(h_ref, wlr_ref, wg_ref, b_ref, o_ref):
    lr = jnp.dot(h_ref[...], wlr_ref[...], preferred_element_type=F32)
    z = jnp.dot(lr, wg_ref[...], precision=HIGHEST, preferred_element_type=F32) + b_ref[...]
    log_sig = jnp.minimum(z, 0.0) - jnp.log1p(jnp.exp(-jnp.abs(z)))
    o_ref[...] = log_sig * (1.0 / GLA_TAU)


def _log_decay(h, w_lr, w_gate, b_gate, tm=512):
    T, D = h.shape
    w_lr = jnp.pad(w_lr, ((0, 0), (0, LANES - GLA_GATE_RANK)))
    w_gate = jnp.pad(w_gate, ((0, LANES - GLA_GATE_RANK), (0, 0)))
    return pl.pallas_call(
        _loga_kernel,
        grid=(T // tm,),
        in_specs=[pl.BlockSpec((tm, D), lambda t: (t, 0)),
                  pl.BlockSpec((D, LANES), lambda t: (0, 0)),
                  pl.BlockSpec((LANES, GLA_WK), lambda t: (0, 0)),
                  pl.BlockSpec((1, GLA_WK), lambda t: (0, 0))],
        out_specs=pl.BlockSpec((tm, GLA_WK), lambda t: (t, 0)),
        out_shape=jax.ShapeDtypeStruct((T, GLA_WK), F32),
        compiler_params=_params("parallel"),
        name="log_decay",
    )(h, w_lr, w_gate, b_gate.reshape(1, GLA_WK))


def _gla_kernel(q_ref, k_ref, v_ref, la_ref, sg_ref, g_ref, o_ref, st_ref, *, chunks):
    @pl.when(pl.program_id(2) == 0)
    def _():
        st_ref[...] = jnp.zeros_like(st_ref)

    C = GLA_CHUNK
    row = lax.broadcasted_iota(jnp.int32, (C, C), 0)
    col = lax.broadcasted_iota(jnp.int32, (C, C), 1)
    causal = col <= row
    tril = causal.astype(F32)
    scale = GLA_DK ** -0.5
    for c in range(chunks):
        rows = slice(c * C, (c + 1) * C)
        b = jnp.dot(tril, la_ref[rows, :], precision=HIGHEST, preferred_element_type=F32)
        b_last = b[C - 1:C, :]
        q = q_ref[rows, :].astype(F32) * scale
        k = k_ref[rows, :].astype(F32)
        v = v_ref[rows, :]
        q_dec = (q * jnp.exp(b)).astype(BF16)
        k_inv_t = (k * jnp.exp(-b)).T.astype(BF16)
        k_rem_t = (k * jnp.exp(b_last - b)).T.astype(BF16)
        dec_t = jnp.broadcast_to(jnp.exp(b_last), (C, GLA_DK)).T
        attn = jnp.dot(q_dec, k_inv_t, preferred_element_type=F32)
        attn = jnp.where(causal, attn, 0.0).astype(BF16)
        st = st_ref[...]
        o = (jnp.dot(q_dec, st.astype(BF16), preferred_element_type=F32)
             + jnp.dot(attn, v, preferred_element_type=F32))
        st_ref[...] = (st * jnp.concatenate([dec_t] * (GLA_DV // C), axis=1)
                       + jnp.dot(k_rem_t, v, preferred_element_type=F32))
        ms = jnp.mean(o * o, axis=-1, keepdims=True)
        y = o * lax.rsqrt(ms + NORM_EPS) * g_ref[...]
        o_ref[rows, :] = (y * sg_ref[rows, :].astype(F32)).astype(o_ref.dtype)


def _gla(qkv, log_a, silu, out_g, B, S, tb=512):
    T = B * S
    nt = S // tb
    kq = GLA_WK // GLA_DK
    kv = 2 * GLA_WK // GLA_DV
    tok = lambda b, h, i: b * nt + i
    return pl.pallas_call(
        functools.partial(_gla_kernel, chunks=tb // GLA_CHUNK),
        grid=(B, GLA_HEADS, nt),
        in_specs=[pl.BlockSpec((tb, GLA_DK), lambda b, h, i: (tok(b, h, i), h)),
                  pl.BlockSpec((tb, GLA_DK), lambda b, h, i: (tok(b, h, i), kq + h)),
                  pl.BlockSpec((tb, GLA_DV), lambda b, h, i: (tok(b, h, i), kv + h)),
                  pl.BlockSpec((tb, GLA_DK), lambda b, h, i: (tok(b, h, i), h)),
                  pl.BlockSpec((tb, GLA_DV), lambda b, h, i: (tok(b, h, i), h)),
                  pl.BlockSpec((1, GLA_DV), lambda b, h, i: (0, 0))],
        out_specs=pl.BlockSpec((tb, GLA_DV), lambda b, h, i: (tok(b, h, i), h)),
        out_shape=jax.ShapeDtypeStruct((T, GLA_WV), BF16),
        scratch_shapes=[pltpu.VMEM((GLA_DK, GLA_DV), F32)],
        compiler_params=_params("parallel", "parallel", "arbitrary"),
        name="gla",
    )(qkv, qkv, qkv, log_a, silu, out_g.reshape(1, GLA_DV))


def _split_bf16(v):
    hi = lax.bitcast_convert_type(lax.bitcast_convert_type(v, jnp.uint32) & jnp.uint32(0xFFFF0000), F32)
    return hi, v - hi


def _key_extras(S):
    nb = S // MOBA_BLOCK
    assert nb <= MOBA_ROUTE_ROWS and S <= LANES * LANES
    pos = jnp.arange(S, dtype=jnp.int32)
    p_hi = ((pos // LANES) * LANES).astype(F32)
    p_lo = (pos % LANES).astype(F32)
    onehot = (jnp.arange(MOBA_ROUTE_ROWS)[None, :] == (jnp.arange(S) // MOBA_BLOCK)[:, None]).astype(F32)
    one = jnp.ones((S,), F32)
    tail = jnp.stack([p_hi, p_hi, p_lo, p_lo, one, one], axis=1)
    pad = jnp.zeros((S, MOBA_DH - MOBA_ROUTE_ROWS - tail.shape[1]), F32)
    return jnp.concatenate([onehot, tail, pad], axis=1).astype(BF16)


def _moba_kernel(slope_ref, qt_ref, k_ref, vg_ref, vo_ref, sm_ref, o_ref,
                 kmean_ref, m_ref, l_ref, acc_ref, s0_ref, s1_ref, *, group):
    hd = pl.program_id(1)
    qb = pl.program_id(2)
    blk = MOBA_BLOCK
    rr = MOBA_ROUTE_ROWS
    slope = slope_ref[hd]

    @pl.when(qb == 0)
    def _():
        kmean_ref[...] = jnp.zeros_like(kmean_ref)

    q_t = qt_ref[...]

    jidx = lax.broadcasted_iota(jnp.int32, (rr, blk), 0)
    jf = jidx.astype(F32)
    past = jidx < qb
    gate = jnp.dot(kmean_ref[...], q_t.astype(F32), precision=HIGHEST, preferred_element_type=F32)
    gate = jnp.where(past, gate, -jnp.inf)
    sel = jnp.zeros((rr, blk), F32)
    for _ in range(MOBA_TOPK):
        best = jnp.max(gate, axis=0, keepdims=True)
        first = jnp.min(jnp.where(gate == best, jf, float(rr)), axis=0, keepdims=True)
        pick = jf == first
        sel = jnp.where(pick, 1.0, sel)
        gate = jnp.where(pick, -jnp.inf, gate)
    route = jnp.where((sel > 0.0) & past, 0.0, NEG_BIG)

    xr = MOBA_DH - rr
    ridx = lax.broadcasted_iota(jnp.int32, (xr, blk), 0)
    s_hi, s_lo = _split_bf16(jnp.full((xr, blk), slope, F32))
    c_hi, c_lo = _split_bf16(jnp.full((xr, blk), -slope * (qb * blk).astype(F32), F32))
    xrows = jnp.where((ridx == 0) | (ridx == 2), s_hi,
                      jnp.where((ridx == 1) | (ridx == 3), s_lo,
                                jnp.where(ridx == 4, c_hi, jnp.where(ridx == 5, c_lo, 0.0))))
    xrows = xrows.astype(BF16)
    q_hat = jnp.concatenate([q_t, route.astype(BF16), xrows], axis=0)
    q_own = jnp.concatenate([q_t, jnp.zeros((rr, blk), BF16), xrows], axis=0)

    key = lax.broadcasted_iota(jnp.int32, (blk, blk), 0)
    qry = lax.broadcasted_iota(jnp.int32, (blk, blk), 1)
    k_own = k_ref[pl.ds(pl.multiple_of(qb * blk, blk), blk), :]
    kmean_ref[pl.ds(qb, 1), :] = jnp.mean(k_own[:, :MOBA_DH].astype(F32), axis=0, keepdims=True)
    s = jnp.dot(k_own, q_own, preferred_element_type=F32)
    s = jnp.where(key <= qry, s, NEG_BIG)
    m0 = jnp.max(s, axis=0, keepdims=True)
    p = jnp.exp(s - m0)
    m_ref[...] = m0
    l_ref[...] = jnp.sum(p, axis=0, keepdims=True)
    acc_ref[...] = jnp.dot(vo_ref[...], p.astype(BF16), preferred_element_type=F32)

    half = group * blk // 2
    n_groups = (qb + group - 1) // group

    def scores(row0):
        k_h = k_ref[pl.ds(pl.multiple_of(row0, half), half), :]
        return jnp.dot(k_h, q_hat, preferred_element_type=F32)

    def fold(s_ref, v_h):
        s = s_ref[...]
        m_old = m_ref[...]
        m_new = jnp.maximum(m_old, jnp.max(s, axis=0, keepdims=True))
        alpha = jnp.exp(m_old - m_new)
        p = jnp.exp(s - m_new)
        l_ref[...] = alpha * l_ref[...] + jnp.sum(p, axis=0, keepdims=True)
        acc_ref[...] = alpha * acc_ref[...] + jnp.dot(v_h, p.astype(BF16), preferred_element_type=F32)
        m_ref[...] = m_new

    s0_ref[...] = scores(0)

    def past_group(gi, carry):
        row0 = gi * (2 * half)
        s1_ref[...] = scores(row0 + half)
        fold(s0_ref, vg_ref[gi, :, 0:half])
        s0_ref[...] = scores(jnp.minimum(gi + 1, n_groups - 1) * (2 * half))
        fold(s1_ref, vg_ref[gi, :, half:2 * half])
        return carry

    lax.fori_loop(0, n_groups, past_group, 0)

    o_t = acc_ref[...] / l_ref[...]
    o_ref[...] = (o_t.T * sm_ref[...].astype(F32)).astype(o_ref.dtype)


def _moba(q_t, k_hat, v_grp, v_own, silu, silu_col0, slopes):
    B, nb, W, blk = q_t.shape
    S = nb * blk
    dh = MOBA_DH
    c0 = silu_col0 // dh
    ng, grows = v_grp.shape[1], v_grp.shape[3]
    return pl.pallas_call(
        functools.partial(_moba_kernel, group=grows // blk),
        grid=(B, MOBA_HEADS, nb),
        in_specs=[pl.BlockSpec(memory_space=pltpu.SMEM),
                  pl.BlockSpec((None, None, dh, blk), lambda b, h, i: (b, i, h, 0)),
                  pl.BlockSpec((None, S, MOBA_KHAT), lambda b, h, i: (b, 0, h)),
                  pl.BlockSpec((None, ng, dh, grows), lambda b, h, i: (b, 0, h, 0)),
                  pl.BlockSpec((None, None, dh, blk), lambda b, h, i: (b, i, h, 0)),
                  pl.BlockSpec((None, blk, dh), lambda b, h, i: (b, i, c0 + h))],
        out_specs=pl.BlockSpec((None, blk, dh), lambda b, h, i: (b, i, h)),
        out_shape=jax.ShapeDtypeStruct((B, S, W), BF16),
        scratch_shapes=[pltpu.VMEM((MOBA_ROUTE_ROWS, dh), F32),
                        pltpu.VMEM((1, blk), F32),
                        pltpu.VMEM((1, blk), F32),
                        pltpu.VMEM((dh, blk), F32),
                        pltpu.VMEM((grows // 2, blk), F32),
                        pltpu.VMEM((grows // 2, blk), F32)],
        compiler_params=_params("parallel", "parallel", "arbitrary"),
        name="moba",
    )(slopes, q_t, k_hat, v_grp, v_own, silu)


def _tail_kernel(ya_ref, yb_ref, ga_ref, gb_ref, x_ref, wa_ref, wb_ref, wo_ref, o_ref):
    za = jnp.dot(ya_ref[...], wa_ref[...], preferred_element_type=F32)
    zb = jnp.dot(yb_ref[...], wb_ref[...], preferred_element_type=F32)
    merged = ga_ref[...].astype(F32) * za + gb_ref[...].astype(F32) * zb
    o_ref[...] = x_ref[...] + jnp.dot(merged.astype(BF16), wo_ref[...], preferred_element_type=F32)


def _tail(ya, yb, gates, x, wa, wb, wo, tm=256):
    T, D = x.shape
    const = lambda t: (0, 0)
    return pl.pallas_call(
        _tail_kernel,
        grid=(T // tm,),
        in_specs=[pl.BlockSpec((tm, GLA_WV), lambda t: (t, 0)),
                  pl.BlockSpec((tm, MOBA_W), lambda t: (t, 0)),
                  pl.BlockSpec((tm, D), lambda t: (t, 0)),
                  pl.BlockSpec((tm, D), lambda t: (t, 1)),
                  pl.BlockSpec((tm, D), lambda t: (t, 0)),
                  pl.BlockSpec((GLA_WV, D), const),
                  pl.BlockSpec((MOBA_W, D), const),
                  pl.BlockSpec((D, D), const)],
        out_specs=pl.BlockSpec((tm, D), lambda t: (t, 0)),
        out_shape=jax.ShapeDtypeStruct((T, D), F32),
        compiler_params=_params("parallel"),
        name="merge_out",
    )(ya, yb, gates, gates, x, wa, wb, wo)


def kernel(x, norm_g, w_in, w_gla_gate, b_gla_gate, gla_out_g, q_norm_g, k_norm_g,
           w_branch_gla, w_branch_moba, w_out):
    B, S, D = x.shape
    T = B * S
    sizes = (GLA_WK, GLA_WK, GLA_WV, GLA_GATE_RANK, GLA_WV, MOBA_W, MOBA_W, MOBA_W, MOBA_W, D, D)
    offs = [0]
    for n in sizes:
        offs.append(offs[-1] + n)
    slopes = jnp.exp2(-8.0 * jnp.arange(1, MOBA_HEADS + 1, dtype=F32) / MOBA_HEADS)

    for layer in range(norm_g.shape[0]):
        w = w_in[layer]
        cols = lambda i, j=None: w[:, offs[i]:offs[(i if j is None else j) + 1]]
        w_qkv = cols(0, 2).astype(BF16)
        w_lr = cols(3).astype(BF16)
        w_silu = jnp.concatenate([cols(4), cols(8)], axis=1).astype(BF16)
        w_mq_t = cols(5).T.astype(BF16)
        w_mk = cols(6).astype(BF16)
        w_mv_t = cols(7).T.astype(BF16)
        w_gates = cols(9, 10).astype(BF16)

        h, h_t = _rms_norm_both(x, norm_g[layer])
        h = h.reshape(T, D)

        qkv = _proj(h, w_qkv, lambda a: a)
        log_a = _log_decay(h, w_lr, w_gla_gate[layer], b_gla_gate[layer])
        silu = _proj(h, w_silu, _silu)
        gates = _proj(h, w_gates, jax.nn.sigmoid)
        k_hat = _proj_khat(h, w_mk, k_norm_g[layer], _key_extras(S))
        (m_q_t,) = _proj_t(w_mq_t, h_t, q_norm_g[layer], True, (MOBA_BLOCK,), scale=MOBA_DH ** -0.5)
        v_own, v_grp = _proj_t(w_mv_t, h_t, q_norm_g[layer], False,
                               (MOBA_BLOCK, min(MOBA_GROUP * MOBA_BLOCK, S)))

        ya = _gla(qkv, log_a, silu, gla_out_g[layer], B, S)
        yb = _moba(m_q_t, k_hat.reshape(B, S, MOBA_HEADS * MOBA_KHAT), v_grp, v_own,
                   silu.reshape(B, S, 2 * GLA_WV), GLA_WV, slopes)

        x = _tail(ya, yb.reshape(T, MOBA_W), gates, x.reshape(T, D),
                  w_branch_gla[layer].astype(BF16), w_branch_moba[layer].astype(BF16),
                  w_out[layer].astype(BF16)).reshape(B, S, D)
    return x
```

```python
import functools

import jax
import jax.numpy as jnp
from jax import lax
from jax.experimental import pallas as pl
from jax.experimental.pallas import tpu as pltpu

F32 = jnp.float32
BF16 = jnp.bfloat16
HIGHEST = lax.Precision.HIGHEST

GLA_HEADS = 4
GLA_DK = 128
GLA_DV = 256
GLA_WK = GLA_HEADS * GLA_DK
GLA_WV = GLA_HEADS * GLA_DV
GLA_GATE_RANK = 16
GLA_TAU = 16.0
GLA_CHUNK = 128

MOBA_HEADS = 8
MOBA_DH = 128
MOBA_W = MOBA_HEADS * MOBA_DH
MOBA_BLOCK = 256
MOBA_TOPK = 3
MOBA_KHAT = 2 * MOBA_DH
MOBA_ROUTE_ROWS = 64
MOBA_GROUP = 4
MOBA_STEPS = 4
MOBA_AHEAD = 3

NORM_EPS = 1e-6
NEG_BIG = -1e30
LANES = 128
VMEM_LIMIT_BYTES = 48 * 1024 * 1024


def _params(*semantics):
    return pltpu.CompilerParams(dimension_semantics=semantics,
                                vmem_limit_bytes=VMEM_LIMIT_BYTES)


def _norm_kernel(x_ref, g_ref, h_ref, ht_ref):
    x = x_ref[...]
    ms = jnp.mean(x * x, axis=-1, keepdims=True)
    y = x * lax.rsqrt(ms + NORM_EPS) * g_ref[...]
    h_ref[...] = y.astype(BF16)
    ht_ref[...] = y.T.astype(BF16)


def _rms_norm_both(x, g, tm=512):
    B, S, D = x.shape
    return pl.pallas_call(
        _norm_kernel,
        grid=(B, S // tm),
        in_specs=[pl.BlockSpec((None, tm, D), lambda b, s: (b, s, 0)),
                  pl.BlockSpec((1, D), lambda b, s: (0, 0))],
        out_specs=[pl.BlockSpec((None, tm, D), lambda b, s: (b, s, 0)),
                   pl.BlockSpec((None, D, tm), lambda b, s: (b, 0, s))],
        out_shape=[jax.ShapeDtypeStruct((B, S, D), BF16),
                   jax.ShapeDtypeStruct((B, D, S), BF16)],
        compiler_params=_params("parallel", "parallel"),
        name="rms_norm",
    )(x, g.reshape(1, D))


def _silu(a):
    return a * jax.nn.sigmoid(a)


def _proj_kernel(h_ref, w_ref, o_ref, *, epilogue):
    acc = jnp.dot(h_ref[...], w_ref[...], preferred_element_type=F32)
    o_ref[...] = epilogue(acc).astype(o_ref.dtype)


def _proj(h, w, epilogue, tm=512, tn=2048):
    T, D = h.shape
    N = w.shape[1]
    tn = min(tn, N)
    return pl.pallas_call(
        functools.partial(_proj_kernel, epilogue=epilogue),
        grid=(N // tn, T // tm),
        in_specs=[pl.BlockSpec((tm, D), lambda n, t: (t, 0)),
                  pl.BlockSpec((D, tn), lambda n, t: (0, n))],
        out_specs=pl.BlockSpec((tm, tn), lambda n, t: (t, n)),
        out_shape=jax.ShapeDtypeStruct((T, N), BF16),
        compiler_params=_params("parallel", "parallel"),
        name="proj",
    )(h, w)


def _proj_khat_kernel(h_ref, w_ref, g_ref, e_ref, o_ref):
    acc = jnp.dot(h_ref[...], w_ref[...], preferred_element_type=F32)
    g = g_ref[...]
    for hd in range(acc.shape[1] // MOBA_DH):
        a = acc[:, hd * MOBA_DH:(hd + 1) * MOBA_DH]
        ms = jnp.mean(a * a, axis=-1, keepdims=True)
        base = hd * MOBA_KHAT
        o_ref[:, base:base + MOBA_DH] = (a * lax.rsqrt(ms + NORM_EPS) * g).astype(o_ref.dtype)
        o_ref[:, base + MOBA_DH:base + MOBA_KHAT] = e_ref[...]


def _proj_khat(h, w, g, extras, tm=512):
    T, D = h.shape
    N = w.shape[1]
    ns = extras.shape[0] // tm
    return pl.pallas_call(
        _proj_khat_kernel,
        grid=(T // tm,),
        in_specs=[pl.BlockSpec((tm, D), lambda t: (t, 0)),
                  pl.BlockSpec((D, N), lambda t: (0, 0)),
                  pl.BlockSpec((1, MOBA_DH), lambda t: (0, 0)),
                  pl.BlockSpec((tm, MOBA_DH), lambda t: (t % ns, 0))],
        out_specs=pl.BlockSpec((tm, MOBA_HEADS * MOBA_KHAT), lambda t: (t, 0)),
        out_shape=jax.ShapeDtypeStruct((T, MOBA_HEADS * MOBA_KHAT), BF16),
        compiler_params=_params("parallel"),
        name="proj_khat",
    )(h, w, g.reshape(1, MOBA_DH), extras)


def _projt_kernel(wt_ref, ht_ref, g_ref, *o_refs, norm, scale):
    acc = jnp.dot(wt_ref[...], ht_ref[...], preferred_element_type=F32)
    tn, tm = acc.shape
    if norm:
        a = acc.reshape(tn // MOBA_DH, MOBA_DH, tm)
        ms = jnp.mean(a * a, axis=1, keepdims=True)
        a = a * lax.rsqrt(ms + NORM_EPS) * (g_ref[...] * scale)[None]
        acc = a.reshape(tn, tm)
    for o_ref in o_refs:
        blk = o_ref.shape[-1]
        for i in range(tm // blk):
            o_ref[i] = acc[:, i * blk:(i + 1) * blk].astype(o_ref.dtype)


def _proj_t(wt, ht, g, norm, blks, scale=1.0, tn=512, tm=1024):
    N, D = wt.shape
    B, _, S = ht.shape
    tm = min(max(tm, *blks), S)
    return pl.pallas_call(
        functools.partial(_projt_kernel, norm=norm, scale=scale),
        grid=(B, S // tm, N // tn),
        in_specs=[pl.BlockSpec((tn, D), lambda b, s, n: (n, 0)),
                  pl.BlockSpec((None, D, tm), lambda b, s, n: (b, 0, s)),
                  pl.BlockSpec((MOBA_DH, 1), lambda b, s, n: (0, 0))],
        out_specs=[pl.BlockSpec((None, tm // blk, tn, blk), lambda b, s, n: (b, s, n, 0)) for blk in blks],
        out_shape=[jax.ShapeDtypeStruct((B, S // blk, N, blk), BF16) for blk in blks],
        compiler_params=_params("parallel", "parallel", "parallel"),
        name="proj_t",
    )(wt, ht, g.reshape(MOBA_DH, 1))


def _loga_kernel(h_ref, wlr_ref, wg_ref, b_ref, o_ref):
    lr = jnp.dot(h_ref[...], wlr_ref[...], preferred_element_type=F32)
    z = jnp.dot(lr, wg_ref[...], precision=HIGHEST, preferred_element_type=F32) + b_ref[...]
    log_sig = jnp.minimum(z, 0.0) - jnp.log1p(jnp.exp(-jnp.abs(z)))
    o_ref[...] = log_sig * (1.0 / GLA_TAU)


def _log_decay(h, w_lr, w_gate, b_gate, tm=512):
    T, D = h.shape
    w_lr = jnp.pad(w_lr, ((0, 0), (0, LANES - GLA_GATE_RANK)))
    w_gate = jnp.pad(w_gate, ((0, LANES - GLA_GATE_RANK), (0, 0)))
    return pl.pallas_call(
        _loga_kernel,
        grid=(T // tm,),
        in_specs=[pl.BlockSpec((tm, D), lambda t: (t, 0)),
                  pl.BlockSpec((D, LANES), lambda t: (0, 0)),
                  pl.BlockSpec((LANES, GLA_WK), lambda t: (0, 0)),
                  pl.BlockSpec((1, GLA_WK), lambda t: (0, 0))],
        out_specs=pl.BlockSpec((tm, GLA_WK), lambda t: (t, 0)),
        out_shape=jax.ShapeDtypeStruct((T, GLA_WK), F32),
        compiler_params=_params("parallel"),
        name="log_decay",
    )(h, w_lr, w_gate, b_gate.reshape(1, GLA_WK))


def _gla_kernel(q_ref, k_ref, v_ref, la_ref, sg_ref, g_ref, o_ref, st_ref, *, chunks):
    @pl.when(pl.program_id(2) == 0)
    def _():
        st_ref[...] = jnp.zeros_like(st_ref)

    C = GLA_CHUNK
    row = lax.broadcasted_iota(jnp.int32, (C, C), 0)
    col = lax.broadcasted_iota(jnp.int32, (C, C), 1)
    causal = col <= row
    tril = causal.astype(F32)
    scale = GLA_DK ** -0.5
    for c in range(chunks):
        rows = slice(c * C, (c + 1) * C)
        b = jnp.dot(tril, la_ref[rows, :], precision=HIGHEST, preferred_element_type=F32)
        b_last = b[C - 1:C, :]
        q = q_ref[rows, :].astype(F32) * scale
        k = k_ref[rows, :].astype(F32)
        v = v_ref[rows, :]
        q_dec = (q * jnp.exp(b)).astype(BF16)
        k_inv_t = (k * jnp.exp(-b)).T.astype(BF16)
        k_rem_t = (k * jnp.exp(b_last - b)).T.astype(BF16)
        dec_t = jnp.broadcast_to(jnp.exp(b_last), (C, GLA_DK)).T
        attn = jnp.dot(q_dec, k_inv_t, preferred_element_type=F32)
        attn = jnp.where(causal, attn, 0.0).astype(BF16)
        st = st_ref[...]
        o = (jnp.dot(q_dec, st.astype(BF16), preferred_element_type=F32)
             + jnp.dot(attn, v, preferred_element_type=F32))
        st_ref[...] = (st * jnp.concatenate([dec_t] * (GLA_DV // C), axis=1)
                       + jnp.dot(k_rem_t, v, preferred_element_type=F32))
        ms = jnp.mean(o * o, axis=-1, keepdims=True)
        y = o * lax.rsqrt(ms + NORM_EPS) * g_ref[...]
        o_ref[rows, :] = (y * sg_ref[rows, :].astype(F32)).astype(o_ref.dtype)


def _gla(qkv, log_a, silu, out_g, B, S, tb=512):
    T = B * S
    nt = S // tb
    kq = GLA_WK // GLA_DK
    kv = 2 * GLA_WK // GLA_DV
    tok = lambda b, h, i: b * nt + i
    return pl.pallas_call(
        functools.partial(_gla_kernel, chunks=tb // GLA_CHUNK),
        grid=(B, GLA_HEADS, nt),
        in_specs=[pl.BlockSpec((tb, GLA_DK), lambda b, h, i: (tok(b, h, i), h)),
                  pl.BlockSpec((tb, GLA_DK), lambda b, h, i: (tok(b, h, i), kq + h)),
                  pl.BlockSpec((tb, GLA_DV), lambda b, h, i: (tok(b, h, i), kv + h)),
                  pl.BlockSpec((tb, GLA_DK), lambda b, h, i: (tok(b, h, i), h)),
                  pl.BlockSpec((tb, GLA_DV), lambda b, h, i: (tok(b, h, i), h)),
                  pl.BlockSpec((1, GLA_DV), lambda b, h, i: (0, 0))],
        out_specs=pl.BlockSpec((tb, GLA_DV), lambda b, h, i: (tok(b, h, i), h)),
        out_shape=jax.ShapeDtypeStruct((T, GLA_WV), BF16),
        scratch_shapes=[pltpu.VMEM((GLA_DK, GLA_DV), F32)],
        compiler_params=_params("parallel", "parallel", "arbitrary"),
        name="gla",
    )(qkv, qkv, qkv, log_a, silu, out_g.reshape(1, GLA_DV))


def _split_bf16(v):
    hi = lax.bitcast_convert_type(lax.bitcast_convert_type(v, jnp.uint32) & jnp.uint32(0xFFFF0000), F32)
    return hi, v - hi


def _key_extras(S):
    nb = S // MOBA_BLOCK
    assert nb <= MOBA_ROUTE_ROWS and S <= LANES * LANES
    pos = jnp.arange(S, dtype=jnp.int32)
    p_hi = ((pos // LANES) * LANES).astype(F32)
    p_lo = (pos % LANES).astype(F32)
    onehot = (jnp.arange(MOBA_ROUTE_ROWS)[None, :] == (jnp.arange(S) // MOBA_BLOCK)[:, None]).astype(F32)
    one = jnp.ones((S,), F32)
    tail = jnp.stack([p_hi, p_hi, p_lo, p_lo, one, one], axis=1)
    pad = jnp.zeros((S, MOBA_DH - MOBA_ROUTE_ROWS - tail.shape[1]), F32)
    return jnp.concatenate([onehot, tail, pad], axis=1).astype(BF16)


def _moba_kernel(slope_ref, qt_ref, k_ref, vg_ref, vo_ref, sm_ref, o_ref,
                 kmean_ref, m_ref, l_ref, a_ref, acc_ref,
                 s0_ref, s1_ref, s2_ref, s3_ref, p0_ref, p1_ref, *, group):
    hd = pl.program_id(1)
    qb = pl.program_id(2)
    blk = MOBA_BLOCK
    rr = MOBA_ROUTE_ROWS
    slope = slope_ref[hd]

    @pl.when(qb == 0)
    def _():
        kmean_ref[...] = jnp.zeros_like(kmean_ref)

    q_t = qt_ref[...]

    jidx = lax.broadcasted_iota(jnp.int32, (rr, blk), 0)
    jf = jidx.astype(F32)
    past = jidx < qb
    gate = jnp.dot(kmean_ref[...], q_t.astype(F32), precision=HIGHEST, preferred_element_type=F32)
    gate = jnp.where(past, gate, -jnp.inf)
    sel = jnp.zeros((rr, blk), F32)
    for _ in range(MOBA_TOPK):
        best = jnp.max(gate, axis=0, keepdims=True)
        first = jnp.min(jnp.where(gate == best, jf, float(rr)), axis=0, keepdims=True)
        pick = jf == first
        sel = jnp.where(pick, 1.0, sel)
        gate = jnp.where(pick, -jnp.inf, gate)
    route = jnp.where((sel > 0.0) & past, 0.0, NEG_BIG)

    xr = MOBA_DH - rr
    ridx = lax.broadcasted_iota(jnp.int32, (xr, blk), 0)
    s_hi, s_lo = _split_bf16(jnp.full((xr, blk), slope, F32))
    c_hi, c_lo = _split_bf16(jnp.full((xr, blk), -slope * (qb * blk).astype(F32), F32))
    xrows = jnp.where((ridx == 0) | (ridx == 2), s_hi,
                      jnp.where((ridx == 1) | (ridx == 3), s_lo,
                                jnp.where(ridx == 4, c_hi, jnp.where(ridx == 5, c_lo, 0.0))))
    xrows = xrows.astype(BF16)
    q_hat = jnp.concatenate([q_t, route.astype(BF16), xrows], axis=0)
    q_own = jnp.concatenate([q_t, jnp.zeros((rr, blk), BF16), xrows], axis=0)

    key = lax.broadcasted_iota(jnp.int32, (blk, blk), 0)
    qry = lax.broadcasted_iota(jnp.int32, (blk, blk), 1)
    k_own = k_ref[pl.ds(pl.multiple_of(qb * blk, blk), blk), :]
    kmean_ref[pl.ds(qb, 1), :] = jnp.mean(k_own[:, :MOBA_DH].astype(F32), axis=0, keepdims=True)
    s = jnp.dot(k_own, q_own, preferred_element_type=F32)
    s = jnp.where(key <= qry, s, NEG_BIG)
    m0 = jnp.max(s, axis=0, keepdims=True)
    p = jnp.exp(s - m0)
    m_ref[...] = m0
    l_ref[...] = jnp.sum(p, axis=0, keepdims=True)
    acc_ref[...] = jnp.dot(vo_ref[...], p.astype(BF16), preferred_element_type=F32)

    unit = group * blk // MOBA_STEPS
    n_groups = (qb + group - 1) // group
    last_unit = jnp.maximum(n_groups * MOBA_STEPS - 1, 0)
    s_refs = (s0_ref, s1_ref, s2_ref, s3_ref)
    p_refs = (p0_ref, p1_ref)

    def scores(t):
        row0 = jnp.minimum(t, last_unit) * unit
        k_u = k_ref[pl.ds(pl.multiple_of(row0, unit), unit), :]
        return jnp.dot(k_u, q_hat, preferred_element_type=F32)

    def pv(v_u, p_ref):
        return jnp.dot(v_u, p_ref[...], preferred_element_type=F32)

    for t in range(MOBA_AHEAD):
        s_refs[t][...] = scores(t)
    p1_ref[...] = jnp.zeros_like(p1_ref)
    a_ref[...] = jnp.ones_like(a_ref)

    def past_group(gi, carry):
        m, l_sum, pending, acc = m_ref[...], l_ref[...], a_ref[...], acc_ref[...]
        prev = jnp.maximum(gi - 1, 0)
        for k in range(MOBA_STEPS):
            s_refs[(k + MOBA_AHEAD) % MOBA_STEPS][...] = scores(gi * MOBA_STEPS + k + MOBA_AHEAD)
            s = s_refs[k][...]
            m_new = jnp.maximum(m, jnp.max(s, axis=0, keepdims=True))
            alpha = jnp.exp(m - m_new)
            p = jnp.exp(s - m_new)
            p_refs[k % 2][...] = p.astype(BF16)
            l_sum = alpha * l_sum + jnp.sum(p, axis=0, keepdims=True)
            v_prev = (vg_ref[prev, :, (MOBA_STEPS - 1) * unit:MOBA_STEPS * unit] if k == 0
                      else vg_ref[gi, :, (k - 1) * unit:k * unit])
            acc = pending * acc + pv(v_prev, p_refs[(k - 1) % 2])
            m, pending = m_new, alpha
        m_ref[...], l_ref[...], a_ref[...], acc_ref[...] = m, l_sum, pending, acc
        return carry

    lax.fori_loop(0, n_groups, past_group, 0)

    last = jnp.maximum(n_groups - 1, 0)
    acc = a_ref[...] * acc_ref[...] + pv(vg_ref[last, :, (MOBA_STEPS - 1) * unit:MOBA_STEPS * unit], p1_ref)
    o_t = acc / l_ref[...]
    o_ref[...] = (o_t.T * sm_ref[...].astype(F32)).astype(o_ref.dtype)


def _moba(q_t, k_hat, v_grp, v_own, silu, silu_col0, slopes):
    B, nb, W, blk = q_t.shape
    S = nb * blk
    dh = MOBA_DH
    c0 = silu_col0 // dh
    ng, grows = v_grp.shape[1], v_grp.shape[3]
    unit = grows // MOBA_STEPS
    return pl.pallas_call(
        functools.partial(_moba_kernel, group=grows // blk),
        grid=(B, MOBA_HEADS, nb),
        in_specs=[pl.BlockSpec(memory_space=pltpu.SMEM),
                  pl.BlockSpec((None, None, dh, blk), lambda b, h, i: (b, i, h, 0)),
                  pl.BlockSpec((None, S, MOBA_KHAT), lambda b, h, i: (b, 0, h)),
                  pl.BlockSpec((None, ng, dh, grows), lambda b, h, i: (b, 0, h, 0)),
                  pl.BlockSpec((None, None, dh, blk), lambda b, h, i: (b, i, h, 0)),
                  pl.BlockSpec((None, blk, dh), lambda b, h, i: (b, i, c0 + h))],
        out_specs=pl.BlockSpec((None, blk, dh), lambda b, h, i: (b, i, h)),
        out_shape=jax.ShapeDtypeStruct((B, S, W), BF16),
        scratch_shapes=[pltpu.VMEM((MOBA_ROUTE_ROWS, dh), F32),
                        pltpu.VMEM((1, blk), F32),
                        pltpu.VMEM((1, blk), F32),
                        pltpu.VMEM((1, blk), F32),
                        pltpu.VMEM((dh, blk), F32)]
                       + [pltpu.VMEM((unit, blk), F32)] * MOBA_STEPS
                       + [pltpu.VMEM((unit, blk), BF16)] * 2,
        compiler_params=_params("parallel", "parallel", "arbitrary"),
        name="moba",
    )(slopes, q_t, k_hat, v_grp, v_own, silu)


def _tail_kernel(ya_ref, yb_ref, ga_ref, gb_ref, x_ref, wa_ref, wb_ref, wo_ref, o_ref):
    za = jnp.dot(ya_ref[...], wa_ref[...], preferred_element_type=F32)
    zb = jnp.dot(yb_ref[...], wb_ref[...], preferred_element_type=F32)
    merged = ga_ref[...].astype(F32) * za + gb_ref[...].astype(F32) * zb
    o_ref[...] = x_ref[...] + jnp.dot(merged.astype(BF16), wo_ref[...], preferred_element_type=F32)


def _tail(ya, yb, gates, x, wa, wb, wo, tm=256):
    T, D = x.shape
    const = lambda t: (0, 0)
    return pl.pallas_call(
        _tail_kernel,
        grid=(T // tm,),
        in_specs=[pl.BlockSpec((tm, GLA_WV), lambda t: (t, 0)),
                  pl.BlockSpec((tm, MOBA_W), lambda t: (t, 0)),
                  pl.BlockSpec((tm, D), lambda t: (t, 0)),
                  pl.BlockSpec((tm, D), lambda t: (t, 1)),
                  pl.BlockSpec((tm, D), lambda t: (t, 0)),
                  pl.BlockSpec((GLA_WV, D), const),
                  pl.BlockSpec((MOBA_W, D), const),
                  pl.BlockSpec((D, D), const)],
        out_specs=pl.BlockSpec((tm, D), lambda t: (t, 0)),
        out_shape=jax.ShapeDtypeStruct((T, D), F32),
        compiler_params=_params("parallel"),
        name="merge_out",
    )(ya, yb, gates, gates, x, wa, wb, wo)


def kernel(x, norm_g, w_in, w_gla_gate, b_gla_gate, gla_out_g, q_norm_g, k_norm_g,
           w_branch_gla, w_branch_moba, w_out):
    B, S, D = x.shape
    T = B * S
    sizes = (GLA_WK, GLA_WK, GLA_WV, GLA_GATE_RANK, GLA_WV, MOBA_W, MOBA_W, MOBA_W, MOBA_W, D, D)
    offs = [0]
    for n in sizes:
        offs.append(offs[-1] + n)
    slopes = jnp.exp2(-8.0 * jnp.arange(1, MOBA_HEADS + 1, dtype=F32) / MOBA_HEADS)

    for layer in range(norm_g.shape[0]):
        w = w_in[layer]
        cols = lambda i, j=None: w[:, offs[i]:offs[(i if j is None else j) + 1]]
        w_qkv = cols(0, 2).astype(BF16)
        w_lr = cols(3).astype(BF16)
        w_silu = jnp.concatenate([cols(4), cols(8)], axis=1).astype(BF16)
        w_mq_t = cols(5).T.astype(BF16)
        w_mk = cols(6).astype(BF16)
        w_mv_t = cols(7).T.astype(BF16)
        w_gates = cols(9, 10).astype(BF16)

        h, h_t = _rms_norm_both(x, norm_g[layer])
        h = h.reshape(T, D)

        qkv = _proj(h, w_qkv, lambda a: a)
        log_a = _log_decay(h, w_lr, w_gla_gate[layer], b_gla_gate[layer])
        silu = _proj(h, w_silu, _silu)
        gates = _proj(h, w_gates, jax.nn.sigmoid)
        k_hat = _proj_khat(h, w_mk, k_norm_g[layer], _key_extras(S))
        (m_q_t,) = _proj_t(w_mq_t, h_t, q_norm_g[layer], True, (MOBA_BLOCK,), scale=MOBA_DH ** -0.5)
        v_own, v_grp = _proj_t(w_mv_t, h_t, q_norm_g[layer], False,
                               (MOBA_BLOCK, min(MOBA_GROUP * MOBA_BLOCK, S)))

        ya = _gla(qkv, log_a, silu, gla_out_g[layer], B, S)
        yb = _moba(m_q_t, k_hat.reshape(B, S, MOBA_HEADS * MOBA_KHAT), v_grp, v_own,
                   silu.reshape(B, S, 2 * GLA_WV), GLA_WV, slopes)

        x = _tail(ya, yb.reshape(T, MOBA_W), gates, x.reshape(T, D),
                  w_branch_gla[layer].astype(BF16), w_branch_moba[layer].astype(BF16),
                  w_out[layer].astype(BF16)).reshape(B, S, D)
    return x
```

```python
import functools

import jax
import jax.numpy as jnp
from jax import lax
from jax.experimental import pallas as pl
from jax.experimental.pallas import tpu as pltpu

F32 = jnp.float32
BF16 = jnp.bfloat16

GLA_HEADS = 4
GLA_DK = 128
GLA_DV = 256
GLA_WK = GLA_HEADS * GLA_DK
GLA_WV = GLA_HEADS * GLA_DV
GLA_GATE_RANK = 16
GLA_TAU = 16.0
GLA_CHUNK = 128

MOBA_HEADS = 8
MOBA_DH = 128
MOBA_W = MOBA_HEADS * MOBA_DH
MOBA_BLOCK = 256
MOBA_TOPK = 3
MOBA_KHAT = 2 * MOBA_DH
MOBA_ROUTE_ROWS = 64
MOBA_GROUP = 4
MOBA_SUM_ROWS = 16
MOBA_STEPS = 4
MOBA_AHEAD = 3

NORM_EPS = 1e-6
NEG_BIG = -1e30
LANES = 128
VMEM_LIMIT_BYTES = 48 * 1024 * 1024


def _params(*semantics):
    return pltpu.CompilerParams(dimension_semantics=semantics,
                                vmem_limit_bytes=VMEM_LIMIT_BYTES)


def _split_bf16(v):
    hi = lax.bitcast_convert_type(lax.bitcast_convert_type(v, jnp.uint32) & jnp.uint32(0xFFFF0000), F32)
    return hi, v - hi


def _norm_kernel(x_ref, g_ref, h_ref, ht_ref):
    x = x_ref[...]
    ms = jnp.mean(x * x, axis=-1, keepdims=True)
    y = x * lax.rsqrt(ms + NORM_EPS) * g_ref[...]
    h_ref[...] = y.astype(BF16)
    ht_ref[...] = y.T.astype(BF16)


def _rms_norm_both(x, g, tm=512):
    B, S, D = x.shape
    return pl.pallas_call(
        _norm_kernel,
        grid=(B, S // tm),
        in_specs=[pl.BlockSpec((None, tm, D), lambda b, s: (b, s, 0)),
                  pl.BlockSpec((1, D), lambda b, s: (0, 0))],
        out_specs=[pl.BlockSpec((None, tm, D), lambda b, s: (b, s, 0)),
                   pl.BlockSpec((None, D, tm), lambda b, s: (b, 0, s))],
        out_shape=[jax.ShapeDtypeStruct((B, S, D), BF16),
                   jax.ShapeDtypeStruct((B, D, S), BF16)],
        compiler_params=_params("parallel", "parallel"),
        name="rms_norm",
    )(x, g.reshape(1, D))


def _silu(a):
    return a * jax.nn.sigmoid(a)


def _proj_kernel(h_ref, w_ref, o_ref, *, epilogue):
    acc = jnp.dot(h_ref[...], w_ref[...], preferred_element_type=F32)
    o_ref[...] = epilogue(acc).astype(o_ref.dtype)


def _proj(h, w, epilogue, tm=512, tn=2048):
    T, D = h.shape
    N = w.shape[1]
    tn = min(tn, N)
    return pl.pallas_call(
        functools.partial(_proj_kernel, epilogue=epilogue),
        grid=(N // tn, T // tm),
        in_specs=[pl.BlockSpec((tm, D), lambda n, t: (t, 0)),
                  pl.BlockSpec((D, tn), lambda n, t: (0, n))],
        out_specs=pl.BlockSpec((tm, tn), lambda n, t: (t, n)),
        out_shape=jax.ShapeDtypeStruct((T, N), BF16),
        compiler_params=_params("parallel", "parallel"),
        name="proj",
    )(h, w)


def _proj_khat_kernel(h_ref, w_ref, g_ref, e_ref, o_ref):
    acc = jnp.dot(h_ref[...], w_ref[...], preferred_element_type=F32)
    g = g_ref[...]
    for hd in range(acc.shape[1] // MOBA_DH):
        a = acc[:, hd * MOBA_DH:(hd + 1) * MOBA_DH]
        ms = jnp.mean(a * a, axis=-1, keepdims=True)
        base = hd * MOBA_KHAT
        o_ref[:, base:base + MOBA_DH] = (a * lax.rsqrt(ms + NORM_EPS) * g).astype(o_ref.dtype)
        o_ref[:, base + MOBA_DH:base + MOBA_KHAT] = e_ref[...]


def _proj_khat(h, w, g, extras, tm=512):
    T, D = h.shape
    N = w.shape[1]
    ns = extras.shape[0] // tm
    return pl.pallas_call(
        _proj_khat_kernel,
        grid=(T // tm,),
        in_specs=[pl.BlockSpec((tm, D), lambda t: (t, 0)),
                  pl.BlockSpec((D, N), lambda t: (0, 0)),
                  pl.BlockSpec((1, MOBA_DH), lambda t: (0, 0)),
                  pl.BlockSpec((tm, MOBA_DH), lambda t: (t % ns, 0))],
        out_specs=pl.BlockSpec((tm, MOBA_HEADS * MOBA_KHAT), lambda t: (t, 0)),
        out_shape=jax.ShapeDtypeStruct((T, MOBA_HEADS * MOBA_KHAT), BF16),
        compiler_params=_params("parallel"),
        name="proj_khat",
    )(h, w, g.reshape(1, MOBA_DH), extras)


def _projt_kernel(wt_ref, ht_ref, g_ref, *o_refs, norm, scale):
    acc = jnp.dot(wt_ref[...], ht_ref[...], preferred_element_type=F32)
    tn, tm = acc.shape
    if norm:
        a = acc.reshape(tn // MOBA_DH, MOBA_DH, tm)
        ms = jnp.mean(a * a, axis=1, keepdims=True)
        a = a * lax.rsqrt(ms + NORM_EPS) * (g_ref[...] * scale)[None]
        acc = a.reshape(tn, tm)
    for o_ref in o_refs:
        blk = o_ref.shape[-1]
        for i in range(tm // blk):
            o_ref[i] = acc[:, i * blk:(i + 1) * blk].astype(o_ref.dtype)


def _proj_t(wt, ht, g, norm, blks, scale=1.0, tn=512, tm=1024):
    N, D = wt.shape
    B, _, S = ht.shape
    tm = min(max(tm, *blks), S)
    return pl.pallas_call(
        functools.partial(_projt_kernel, norm=norm, scale=scale),
        grid=(B, S // tm, N // tn),
        in_specs=[pl.BlockSpec((tn, D), lambda b, s, n: (n, 0)),
                  pl.BlockSpec((None, D, tm), lambda b, s, n: (b, 0, s)),
                  pl.BlockSpec((MOBA_DH, 1), lambda b, s, n: (0, 0))],
        out_specs=[pl.BlockSpec((None, tm // blk, tn, blk), lambda b, s, n: (b, s, n, 0)) for blk in blks],
        out_shape=[jax.ShapeDtypeStruct((B, S // blk, N, blk), BF16) for blk in blks],
        compiler_params=_params("parallel", "parallel", "parallel"),
        name="proj_t",
    )(wt, ht, g.reshape(MOBA_DH, 1))


def _loga_kernel(h_ref, wlr_ref, wg_ref, b_ref, o_ref):
    lr = jnp.dot(h_ref[...], wlr_ref[...], preferred_element_type=F32)
    lr_hi, lr_lo = _split_bf16(lr)
    lane = lax.broadcasted_iota(jnp.int32, lr.shape, 1)
    lr_pack = jnp.where(lane < 2 * GLA_GATE_RANK, lr_hi, lr_lo).astype(BF16)
    z = jnp.dot(lr_pack, wg_ref[...], preferred_element_type=F32) + b_ref[...]
    log_sig = jnp.minimum(z, 0.0) - jnp.log1p(jnp.exp(-jnp.abs(z)))
    o_ref[...] = log_sig * (1.0 / GLA_TAU)


def _log_decay(h, w_lr, w_gate, b_gate, tm=512):
    T, D = h.shape
    w_lr = jnp.pad(jnp.concatenate([w_lr] * 3, axis=1), ((0, 0), (0, LANES - 3 * GLA_GATE_RANK)))
    g_hi, g_lo = _split_bf16(w_gate)
    w_gate = jnp.pad(jnp.concatenate([g_hi, g_lo, g_hi], axis=0),
                     ((0, LANES - 3 * GLA_GATE_RANK), (0, 0))).astype(BF16)
    return pl.pallas_call(
        _loga_kernel,
        grid=(T // tm,),
        in_specs=[pl.BlockSpec((tm, D), lambda t: (t, 0)),
                  pl.BlockSpec((D, LANES), lambda t: (0, 0)),
                  pl.BlockSpec((LANES, GLA_WK), lambda t: (0, 0)),
                  pl.BlockSpec((1, GLA_WK), lambda t: (0, 0))],
        out_specs=pl.BlockSpec((tm, GLA_WK), lambda t: (t, 0)),
        out_shape=jax.ShapeDtypeStruct((T, GLA_WK), F32),
        compiler_params=_params("parallel"),
        name="log_decay",
    )(h, w_lr, w_gate, b_gate.reshape(1, GLA_WK))


def _gla_kernel(q_ref, k_ref, v_ref, la_ref, sg_ref, g_ref, o_ref, st_ref, *, chunks):
    @pl.when(pl.program_id(1) == 0)
    def _():
        st_ref[...] = jnp.zeros_like(st_ref)

    C = GLA_CHUNK
    row = lax.broadcasted_iota(jnp.int32, (C, C), 0)
    col = lax.broadcasted_iota(jnp.int32, (C, C), 1)
    causal = col <= row
    tril3 = jnp.concatenate([causal.astype(BF16)] * 3, axis=1)
    scale = GLA_DK ** -0.5

    def cumsum(x):
        hi, rest = _split_bf16(x)
        mid, lo = _split_bf16(rest)
        parts = jnp.concatenate([hi, mid, lo], axis=0).astype(BF16)
        return jnp.dot(tril3, parts, preferred_element_type=F32)

    items = [(c, hd) for c in range(chunks) for hd in range(GLA_HEADS)]
    rows = lambda c: slice(c * C, (c + 1) * C)
    kcols = lambda hd: slice(hd * GLA_DK, (hd + 1) * GLA_DK)
    vcols = lambda hd: slice(hd * GLA_DV, (hd + 1) * GLA_DV)

    b = {(c, hd): cumsum(la_ref[rows(c), kcols(hd)]) for c, hd in items}
    q_dec, k_inv_t, k_rem_t, dec_t = {}, {}, {}, {}
    for c, hd in items:
        bi = b[c, hd]
        b_last = bi[C - 1:C, :]
        q = q_ref[rows(c), kcols(hd)].astype(F32) * scale
        k = k_ref[rows(c), kcols(hd)].astype(F32)
        q_dec[c, hd] = (q * jnp.exp(bi)).astype(BF16)
        k_inv_t[c, hd] = (k * jnp.exp(-bi)).T.astype(BF16)
        k_rem_t[c, hd] = (k * jnp.exp(b_last - bi)).T.astype(BF16)
        dec_t[c, hd] = jnp.broadcast_to(jnp.exp(b_last), (C, GLA_DK)).T
    attn = {it: jnp.dot(q_dec[it], k_inv_t[it], preferred_element_type=F32) for it in items}
    attn = {it: jnp.where(causal, attn[it], 0.0).astype(BF16) for it in items}
    o_intra = {(c, hd): jnp.dot(attn[c, hd], v_ref[rows(c), vcols(hd)], preferred_element_type=F32)
               for c, hd in items}
    kv = {(c, hd): jnp.dot(k_rem_t[c, hd], v_ref[rows(c), vcols(hd)], preferred_element_type=F32)
          for c, hd in items}

    for c in range(chunks):
        st = [st_ref[hd] for hd in range(GLA_HEADS)]
        o = [jnp.dot(q_dec[c, hd], st[hd].astype(BF16), preferred_element_type=F32) + o_intra[c, hd]
             for hd in range(GLA_HEADS)]
        for hd in range(GLA_HEADS):
            st_ref[hd] = st[hd] * jnp.concatenate([dec_t[c, hd]] * (GLA_DV // C), axis=1) + kv[c, hd]
            ms = jnp.mean(o[hd] * o[hd], axis=-1, keepdims=True)
            y = o[hd] * lax.rsqrt(ms + NORM_EPS) * g_ref[...]
            o_ref[rows(c), vcols(hd)] = (y * sg_ref[rows(c), vcols(hd)].astype(F32)).astype(o_ref.dtype)


def _gla(qkv, log_a, silu, out_g, B, S, tb=512):
    T = B * S
    nt = S // tb
    tok = lambda b, i: b * nt + i
    return pl.pallas_call(
        functools.partial(_gla_kernel, chunks=tb // GLA_CHUNK),
        grid=(B, nt),
        in_specs=[pl.BlockSpec((tb, GLA_WK), lambda b, i: (tok(b, i), 0)),
                  pl.BlockSpec((tb, GLA_WK), lambda b, i: (tok(b, i), 1)),
                  pl.BlockSpec((tb, GLA_WV), lambda b, i: (tok(b, i), 2 * GLA_WK // GLA_WV)),
                  pl.BlockSpec((tb, GLA_WK), lambda b, i: (tok(b, i), 0)),
                  pl.BlockSpec((tb, GLA_WV), lambda b, i: (tok(b, i), 0)),
                  pl.BlockSpec((1, GLA_DV), lambda b, i: (0, 0))],
        out_specs=pl.BlockSpec((tb, GLA_WV), lambda b, i: (tok(b, i), 0)),
        out_shape=jax.ShapeDtypeStruct((T, GLA_WV), BF16),
        scratch_shapes=[pltpu.VMEM((GLA_HEADS, GLA_DK, GLA_DV), F32)],
        compiler_params=_params("parallel", "arbitrary"),
        name="gla",
    )(qkv, qkv, qkv, log_a, silu, out_g.reshape(1, GLA_DV))


def _key_extras(S):
    nb = S // MOBA_BLOCK
    assert nb <= MOBA_ROUTE_ROWS and S <= LANES * LANES
    pos = jnp.arange(S, dtype=jnp.int32)
    p_hi = ((pos // LANES) * LANES).astype(F32)
    p_lo = (pos % LANES).astype(F32)
    onehot = (jnp.arange(MOBA_ROUTE_ROWS)[None, :] == (jnp.arange(S) // MOBA_BLOCK)[:, None]).astype(F32)
    one = jnp.ones((S,), F32)
    tail = jnp.stack([p_hi, p_hi, p_lo, p_lo, one, one], axis=1)
    pad = jnp.zeros((S, MOBA_DH - MOBA_ROUTE_ROWS - tail.shape[1]), F32)
    return jnp.concatenate([onehot, tail, pad], axis=1).astype(BF16)


def _moba_kernel(slope_ref, qt_ref, k_ref, vg_ref, vo_ref, sm_ref, o_ref,
                 kmean_ref, m_ref, a_ref, acc_ref,
                 s0_ref, s1_ref, s2_ref, s3_ref, p0_ref, p1_ref, *, group):
    hd = pl.program_id(1)
    qb = pl.program_id(2)
    blk = MOBA_BLOCK
    rr = MOBA_ROUTE_ROWS
    slope = slope_ref[hd]

    @pl.when(qb == 0)
    def _():
        kmean_ref[...] = jnp.zeros_like(kmean_ref)

    q_t = qt_ref[...]

    jidx = lax.broadcasted_iota(jnp.int32, (rr, blk), 0)
    jf = jidx.astype(F32)
    past = jidx < qb
    km_hi, km_lo = _split_bf16(kmean_ref[...])
    gate = jnp.dot(jnp.concatenate([km_hi, km_lo], axis=1).astype(BF16),
                   jnp.concatenate([q_t, q_t], axis=0), preferred_element_type=F32)
    gate = jnp.where(past, gate, -jnp.inf)
    sel = jnp.zeros((rr, blk), F32)
    for _ in range(MOBA_TOPK):
        best = jnp.max(gate, axis=0, keepdims=True)
        first = jnp.min(jnp.where(gate == best, jf, float(rr)), axis=0, keepdims=True)
        pick = jf == first
        sel = jnp.where(pick, 1.0, sel)
        gate = jnp.where(pick, -jnp.inf, gate)
    route = jnp.where((sel > 0.0) & past, 0.0, NEG_BIG)

    xr = MOBA_DH - rr
    ridx = lax.broadcasted_iota(jnp.int32, (xr, blk), 0)
    s_hi, s_lo = _split_bf16(jnp.full((xr, blk), slope, F32))
    c_hi, c_lo = _split_bf16(jnp.full((xr, blk), -slope * (qb * blk).astype(F32), F32))
    xrows = jnp.where((ridx == 0) | (ridx == 2), s_hi,
                      jnp.where((ridx == 1) | (ridx == 3), s_lo,
                                jnp.where(ridx == 4, c_hi, jnp.where(ridx == 5, c_lo, 0.0))))
    xrows = xrows.astype(BF16)
    q_hat = jnp.concatenate([q_t, route.astype(BF16), xrows], axis=0)
    q_own = jnp.concatenate([q_t, jnp.zeros((rr, blk), BF16), xrows], axis=0)

    key = lax.broadcasted_iota(jnp.int32, (blk, blk), 0)
    qry = lax.broadcasted_iota(jnp.int32, (blk, blk), 1)
    k_own = k_ref[pl.ds(pl.multiple_of(qb * blk, blk), blk), :]
    kmean_ref[pl.ds(qb, 1), :] = jnp.mean(k_own[:, :MOBA_DH].astype(F32), axis=0, keepdims=True)
    s = jnp.dot(k_own, q_own, preferred_element_type=F32)
    s = jnp.where(key <= qry, s, NEG_BIG)
    m0 = jnp.max(s, axis=0, keepdims=True)
    m_ref[...] = m0
    p0_ref[...] = jnp.exp((s - m0).astype(BF16))

    ones = jnp.ones((MOBA_SUM_ROWS, blk), BF16)

    def pv(v_u, p_ref):
        return jnp.dot(jnp.concatenate([v_u, ones], axis=0), p_ref[...], preferred_element_type=F32)

    acc_ref[...] = pv(vo_ref[...], p0_ref)

    unit = group * blk // MOBA_STEPS
    n_groups = (qb + group - 1) // group
    last_unit = jnp.maximum(n_groups * MOBA_STEPS - 1, 0)
    s_refs = (s0_ref, s1_ref, s2_ref, s3_ref)
    p_refs = (p0_ref, p1_ref)

    def scores(t):
        row0 = jnp.minimum(t, last_unit) * unit
        k_u = k_ref[pl.ds(pl.multiple_of(row0, unit), unit), :]
        return jnp.dot(k_u, q_hat, preferred_element_type=F32)

    for t in range(MOBA_AHEAD):
        s_refs[t][...] = scores(t)
    p1_ref[...] = jnp.zeros_like(p1_ref)
    a_ref[...] = jnp.ones_like(a_ref)

    def past_groups(g0, count):
        m, pending = m_ref[...], a_ref[...]
        for gi in [g0 + d for d in range(count)]:
            prev = jnp.maximum(gi - 1, 0)
            for k in range(MOBA_STEPS):
                s_refs[(k + MOBA_AHEAD) % MOBA_STEPS][...] = scores(gi * MOBA_STEPS + k + MOBA_AHEAD)
                s = s_refs[k][...]
                m_new = jnp.maximum(m, jnp.max(s, axis=0, keepdims=True))
                alpha = jnp.exp(m - m_new)
                p_refs[k % 2][...] = jnp.exp((s - m_new).astype(BF16))
                v_prev = (vg_ref[prev, :, (MOBA_STEPS - 1) * unit:MOBA_STEPS * unit] if k == 0
                          else vg_ref[gi, :, (k - 1) * unit:k * unit])
                acc_ref[...] = pending * acc_ref[...] + pv(v_prev, p_refs[(k - 1) % 2])
                m, pending = m_new, alpha
        m_ref[...], a_ref[...] = m, pending

    def two_groups(i, carry):
        past_groups(2 * i, 2)
        return carry

    lax.fori_loop(0, n_groups // 2, two_groups, 0)

    @pl.when(n_groups % 2 == 1)
    def _():
        past_groups(n_groups - 1, 1)

    last = jnp.maximum(n_groups - 1, 0)
    acc = a_ref[...] * acc_ref[...] + pv(vg_ref[last, :, (MOBA_STEPS - 1) * unit:MOBA_STEPS * unit], p1_ref)
    o_t = acc[:MOBA_DH] / acc[MOBA_DH:MOBA_DH + 1]
    o_ref[...] = (o_t.T * sm_ref[...].astype(F32)).astype(o_ref.dtype)


def _moba(q_t, k_hat, v_grp, v_own, silu, silu_col0, slopes):
    B, nb, W, blk = q_t.shape
    S = nb * blk
    dh = MOBA_DH
    c0 = silu_col0 // dh
    ng, grows = v_grp.shape[1], v_grp.shape[3]
    unit = grows // MOBA_STEPS
    return pl.pallas_call(
        functools.partial(_moba_kernel, group=grows // blk),
        grid=(B, MOBA_HEADS, nb),
        in_specs=[pl.BlockSpec(memory_space=pltpu.SMEM),
                  pl.BlockSpec((None, None, dh, blk), lambda b, h, i: (b, i, h, 0)),
                  pl.BlockSpec((None, S, MOBA_KHAT), lambda b, h, i: (b, 0, h)),
                  pl.BlockSpec((None, ng, dh, grows), lambda b, h, i: (b, 0, h, 0)),
                  pl.BlockSpec((None, None, dh, blk), lambda b, h, i: (b, i, h, 0)),
                  pl.BlockSpec((None, blk, dh), lambda b, h, i: (b, i, c0 + h))],
        out_specs=pl.BlockSpec((None, blk, dh), lambda b, h, i: (b, i, h)),
        out_shape=jax.ShapeDtypeStruct((B, S, W), BF16),
        scratch_shapes=[pltpu.VMEM((MOBA_ROUTE_ROWS, dh), F32),
                        pltpu.VMEM((1, blk), F32),
                        pltpu.VMEM((1, blk), F32),
                        pltpu.VMEM((dh + MOBA_SUM_ROWS, blk), F32)]
                       + [pltpu.VMEM((unit, blk), F32)] * MOBA_STEPS
                       + [pltpu.VMEM((unit, blk), BF16)] * 2,
        compiler_params=_params("parallel", "parallel", "arbitrary"),
        name="moba",
    )(slopes, q_t, k_hat, v_grp, v_own, silu)


def _tail_kernel(ya_ref, yb_ref, ga_ref, gb_ref, x_ref, wa_ref, wb_ref, wo_ref, o_ref):
    za = jnp.dot(ya_ref[...], wa_ref[...], preferred_element_type=F32)
    zb = jnp.dot(yb_ref[...], wb_ref[...], preferred_element_type=F32)
    merged = ga_ref[...].astype(F32) * za + gb_ref[...].astype(F32) * zb
    o_ref[...] = x_ref[...] + jnp.dot(merged.astype(BF16), wo_ref[...], preferred_element_type=F32)


def _tail(ya, yb, gates, x, wa, wb, wo, tm=256):
    T, D = x.shape
    const = lambda t: (0, 0)
    return pl.pallas_call(
        _tail_kernel,
        grid=(T // tm,),
        in_specs=[pl.BlockSpec((tm, GLA_WV), lambda t: (t, 0)),
                  pl.BlockSpec((tm, MOBA_W), lambda t: (t, 0)),
                  pl.BlockSpec((tm, D), lambda t: (t, 0)),
                  pl.BlockSpec((tm, D), lambda t: (t, 1)),
                  pl.BlockSpec((tm, D), lambda t: (t, 0)),
                  pl.BlockSpec((GLA_WV, D), const),
                  pl.BlockSpec((MOBA_W, D), const),
                  pl.BlockSpec((D, D), const)],
        out_specs=pl.BlockSpec((tm, D), lambda t: (t, 0)),
        out_shape=jax.ShapeDtypeStruct((T, D), F32),
        compiler_params=_params("parallel"),
        name="merge_out",
    )(ya, yb, gates, gates, x, wa, wb, wo)


def kernel(x, norm_g, w_in, w_gla_gate, b_gla_gate, gla_out_g, q_norm_g, k_norm_g,
           w_branch_gla, w_branch_moba, w_out):
    B, S, D = x.shape
    T = B * S
    sizes = (GLA_WK, GLA_WK, GLA_WV, GLA_GATE_RANK, GLA_WV, MOBA_W, MOBA_W, MOBA_W, MOBA_W, D, D)
    offs = [0]
    for n in sizes:
        offs.append(offs[-1] + n)
    slopes = jnp.exp2(-8.0 * jnp.arange(1, MOBA_HEADS + 1, dtype=F32) / MOBA_HEADS)

    for layer in range(norm_g.shape[0]):
        w = w_in[layer]
        cols = lambda i, j=None: w[:, offs[i]:offs[(i if j is None else j) + 1]]
        w_qkv = cols(0, 2).astype(BF16)
        w_lr = cols(3).astype(BF16)
        w_silu = jnp.concatenate([cols(4), cols(8)], axis=1).astype(BF16)
        w_mq_t = cols(5).T.astype(BF16)
        w_mk = cols(6).astype(BF16)
        w_mv_t = cols(7).T.astype(BF16)
        w_gates = cols(9, 10).astype(BF16)

        h, h_t = _rms_norm_both(x, norm_g[layer])
        h = h.reshape(T, D)

        qkv = _proj(h, w_qkv, lambda a: a)
        log_a = _log_decay(h, w_lr, w_gla_gate[layer], b_gla_gate[layer])
        silu = _proj(h, w_silu, _silu)
        gates = _proj(h, w_gates, jax.nn.sigmoid)
        k_hat = _proj_khat(h, w_mk, k_norm_g[layer], _key_extras(S))
        (m_q_t,) = _proj_t(w_mq_t, h_t, q_norm_g[layer], True, (MOBA_BLOCK,), scale=MOBA_DH ** -0.5)
        v_own, v_grp = _proj_t(w_mv_t, h_t, q_norm_g[layer], False,
                               (MOBA_BLOCK, min(MOBA_GROUP * MOBA_BLOCK, S)))

        ya = _gla(qkv, log_a, silu, gla_out_g[layer], B, S)
        yb = _moba(m_q_t, k_hat.reshape(B, S, MOBA_HEADS * MOBA_KHAT), v_grp, v_own,
                   silu.reshape(B, S, 2 * GLA_WV), GLA_WV, slopes)

        x = _tail(ya, yb.reshape(T, MOBA_W), gates, x.reshape(T, D),
                  w_branch_gla[layer].astype(BF16), w_branch_moba[layer].astype(BF16),
                  w_out[layer].astype(BF16)).reshape(B, S, D)
    return x
```

```python
import functools

import jax
import jax.numpy as jnp
from jax import lax
from jax.experimental import pallas as pl
from jax.experimental.pallas import tpu as pltpu

F32 = jnp.float32
BF16 = jnp.bfloat16

GLA_HEADS = 4
GLA_DK = 128
GLA_DV = 256
GLA_WK = GLA_HEADS * GLA_DK
GLA_WV = GLA_HEADS * GLA_DV
GLA_GATE_RANK = 16
GLA_TAU = 16.0
GLA_CHUNK = 128

MOBA_HEADS = 8
MOBA_DH = 128
MOBA_W = MOBA_HEADS * MOBA_DH
MOBA_BLOCK = 256
MOBA_TOPK = 3
MOBA_KHAT = 2 * MOBA_DH
MOBA_ROUTE_ROWS = 64
MOBA_GROUP = 4
MOBA_SUM_ROWS = 16
MOBA_STEPS = 4
MOBA_AHEAD = 3

NORM_EPS = 1e-6
NEG_BIG = -1e30
LANES = 128
VMEM_LIMIT_BYTES = 48 * 1024 * 1024


def _params(*semantics):
    return pltpu.CompilerParams(dimension_semantics=semantics,
                                vmem_limit_bytes=VMEM_LIMIT_BYTES)


def _split_bf16(v):
    hi = lax.bitcast_convert_type(lax.bitcast_convert_type(v, jnp.uint32) & jnp.uint32(0xFFFF0000), F32)
    return hi, v - hi


def _norm_kernel(x_ref, g_ref, h_ref, ht_ref):
    x = x_ref[...]
    ms = jnp.mean(x * x, axis=-1, keepdims=True)
    y = x * lax.rsqrt(ms + NORM_EPS) * g_ref[...]
    h_ref[...] = y.astype(BF16)
    ht_ref[...] = y.T.astype(BF16)


def _rms_norm_both(x, g, tm=512):
    B, S, D = x.shape
    return pl.pallas_call(
        _norm_kernel,
        grid=(B, S // tm),
        in_specs=[pl.BlockSpec((None, tm, D), lambda b, s: (b, s, 0)),
                  pl.BlockSpec((1, D), lambda b, s: (0, 0))],
        out_specs=[pl.BlockSpec((None, tm, D), lambda b, s: (b, s, 0)),
                   pl.BlockSpec((None, D, tm), lambda b, s: (b, 0, s))],
        out_shape=[jax.ShapeDtypeStruct((B, S, D), BF16),
                   jax.ShapeDtypeStruct((B, D, S), BF16)],
        compiler_params=_params("parallel", "parallel"),
        name="rms_norm",
    )(x, g.reshape(1, D))


def _silu(a):
    return a * jax.nn.sigmoid(a)


def _proj_kernel(h_ref, w_ref, o_ref, *, epilogue):
    acc = jnp.dot(h_ref[...], w_ref[...], preferred_element_type=F32)
    o_ref[...] = epilogue(acc).astype(o_ref.dtype)


def _proj(h, w, epilogue, tm=512, tn=2048):
    T, D = h.shape
    N = w.shape[1]
    tn = min(tn, N)
    return pl.pallas_call(
        functools.partial(_proj_kernel, epilogue=epilogue),
        grid=(N // tn, T // tm),
        in_specs=[pl.BlockSpec((tm, D), lambda n, t: (t, 0)),
                  pl.BlockSpec((D, tn), lambda n, t: (0, n))],
        out_specs=pl.BlockSpec((tm, tn), lambda n, t: (t, n)),
        out_shape=jax.ShapeDtypeStruct((T, N), BF16),
        compiler_params=_params("parallel", "parallel"),
        name="proj",
    )(h, w)


def _proj_khat_kernel(h_ref, w_ref, g_ref, e_ref, o_ref):
    acc = jnp.dot(h_ref[...], w_ref[...], preferred_element_type=F32)
    g = g_ref[...]
    for hd in range(acc.shape[1] // MOBA_DH):
        a = acc[:, hd * MOBA_DH:(hd + 1) * MOBA_DH]
        ms = jnp.mean(a * a, axis=-1, keepdims=True)
        base = hd * MOBA_KHAT
        o_ref[:, base:base + MOBA_DH] = (a * lax.rsqrt(ms + NORM_EPS) * g).astype(o_ref.dtype)
        o_ref[:, base + MOBA_DH:base + MOBA_KHAT] = e_ref[...]


def _proj_khat(h, w, g, extras, tm=512):
    T, D = h.shape
    N = w.shape[1]
    ns = extras.shape[0] // tm
    return pl.pallas_call(
        _proj_khat_kernel,
        grid=(T // tm,),
        in_specs=[pl.BlockSpec((tm, D), lambda t: (t, 0)),
                  pl.BlockSpec((D, N), lambda t: (0, 0)),
                  pl.BlockSpec((1, MOBA_DH), lambda t: (0, 0)),
                  pl.BlockSpec((tm, MOBA_DH), lambda t: (t % ns, 0))],
        out_specs=pl.BlockSpec((tm, MOBA_HEADS * MOBA_KHAT), lambda t: (t, 0)),
        out_shape=jax.ShapeDtypeStruct((T, MOBA_HEADS * MOBA_KHAT), BF16),
        compiler_params=_params("parallel"),
        name="proj_khat",
    )(h, w, g.reshape(1, MOBA_DH), extras)


def _projt_kernel(wt_ref, ht_ref, g_ref, *o_refs, norm, scale):
    acc = jnp.dot(wt_ref[...], ht_ref[...], preferred_element_type=F32)
    tn, tm = acc.shape
    if norm:
        a = acc.reshape(tn // MOBA_DH, MOBA_DH, tm)
        ms = jnp.mean(a * a, axis=1, keepdims=True)
        a = a * lax.rsqrt(ms + NORM_EPS) * (g_ref[...] * scale)[None]
        acc = a.reshape(tn, tm)
    for o_ref in o_refs:
        blk = o_ref.shape[-1]
        for i in range(tm // blk):
            o_ref[i] = acc[:, i * blk:(i + 1) * blk].astype(o_ref.dtype)


def _proj_t(wt, ht, g, norm, blks, scale=1.0, tn=512, tm=1024):
    N, D = wt.shape
    B, _, S = ht.shape
    tm = min(max(tm, *blks), S)
    return pl.pallas_call(
        functools.partial(_projt_kernel, norm=norm, scale=scale),
        grid=(B, S // tm, N // tn),
        in_specs=[pl.BlockSpec((tn, D), lambda b, s, n: (n, 0)),
                  pl.BlockSpec((None, D, tm), lambda b, s, n: (b, 0, s)),
                  pl.BlockSpec((MOBA_DH, 1), lambda b, s, n: (0, 0))],
        out_specs=[pl.BlockSpec((None, tm // blk, tn, blk), lambda b, s, n: (b, s, n, 0)) for blk in blks],
        out_shape=[jax.ShapeDtypeStruct((B, S // blk, N, blk), BF16) for blk in blks],
        compiler_params=_params("parallel", "parallel", "parallel"),
        name="proj_t",
    )(wt, ht, g.reshape(MOBA_DH, 1))


def _loga_kernel(h_ref, wlr_ref, wg_ref, b_ref, o_ref):
    lr = jnp.dot(h_ref[...], wlr_ref[...], preferred_element_type=F32)
    lr_hi, lr_lo = _split_bf16(lr)
    lane = lax.broadcasted_iota(jnp.int32, lr.shape, 1)
    lr_pack = jnp.where(lane < 2 * GLA_GATE_RANK, lr_hi, lr_lo).astype(BF16)
    z = jnp.dot(lr_pack, wg_ref[...], preferred_element_type=F32) + b_ref[...]
    log_sig = jnp.minimum(z, 0.0) - jnp.log1p(jnp.exp(-jnp.abs(z)))
    o_ref[...] = log_sig * (1.0 / GLA_TAU)


def _log_decay(h, w_lr, w_gate, b_gate, tm=512):
    T, D = h.shape
    w_lr = jnp.pad(jnp.concatenate([w_lr] * 3, axis=1), ((0, 0), (0, LANES - 3 * GLA_GATE_RANK)))
    g_hi, g_lo = _split_bf16(w_gate)
    w_gate = jnp.pad(jnp.concatenate([g_hi, g_lo, g_hi], axis=0),
                     ((0, LANES - 3 * GLA_GATE_RANK), (0, 0))).astype(BF16)
    return pl.pallas_call(
        _loga_kernel,
        grid=(T // tm,),
        in_specs=[pl.BlockSpec((tm, D), lambda t: (t, 0)),
                  pl.BlockSpec((D, LANES), lambda t: (0, 0)),
                  pl.BlockSpec((LANES, GLA_WK), lambda t: (0, 0)),
                  pl.BlockSpec((1, GLA_WK), lambda t: (0, 0))],
        out_specs=pl.BlockSpec((tm, GLA_WK), lambda t: (t, 0)),
        out_shape=jax.ShapeDtypeStruct((T, GLA_WK), F32),
        compiler_params=_params("parallel"),
        name="log_decay",
    )(h, w_lr, w_gate, b_gate.reshape(1, GLA_WK))


def _gla_kernel(q_ref, k_ref, v_ref, la_ref, sg_ref, g_ref, o_ref, st_ref, *, chunks):
    @pl.when(pl.program_id(1) == 0)
    def _():
        st_ref[...] = jnp.zeros_like(st_ref)

    C = GLA_CHUNK
    row = lax.broadcasted_iota(jnp.int32, (C, C), 0)
    col = lax.broadcasted_iota(jnp.int32, (C, C), 1)
    causal = col <= row
    tril3 = jnp.concatenate([causal.astype(BF16)] * 3, axis=1)
    scale = GLA_DK ** -0.5

    def cumsum(x):
        hi, rest = _split_bf16(x)
        mid, lo = _split_bf16(rest)
        parts = jnp.concatenate([hi, mid, lo], axis=0).astype(BF16)
        return jnp.dot(tril3, parts, preferred_element_type=F32)

    items = [(c, hd) for c in range(chunks) for hd in range(GLA_HEADS)]
    rows = lambda c: slice(c * C, (c + 1) * C)
    kcols = lambda hd: slice(hd * GLA_DK, (hd + 1) * GLA_DK)
    vcols = lambda hd: slice(hd * GLA_DV, (hd + 1) * GLA_DV)

    b = {(c, hd): cumsum(la_ref[rows(c), kcols(hd)]) for c, hd in items}
    q_dec, k_inv_t, k_rem_t, dec_t = {}, {}, {}, {}
    for c, hd in items:
        bi = b[c, hd]
        b_last = bi[C - 1:C, :]
        q = q_ref[rows(c), kcols(hd)].astype(F32) * scale
        k = k_ref[rows(c), kcols(hd)].astype(F32)
        q_dec[c, hd] = (q * jnp.exp(bi)).astype(BF16)
        k_inv_t[c, hd] = (k * jnp.exp(-bi)).T.astype(BF16)
        k_rem_t[c, hd] = (k * jnp.exp(b_last - bi)).T.astype(BF16)
        dec_t[c, hd] = jnp.broadcast_to(jnp.exp(b_last), (C, GLA_DK)).T
    attn = {it: jnp.dot(q_dec[it], k_inv_t[it], preferred_element_type=F32) for it in items}
    attn = {it: jnp.where(causal, attn[it], 0.0).astype(BF16) for it in items}
    o_intra = {(c, hd): jnp.dot(attn[c, hd], v_ref[rows(c), vcols(hd)], preferred_element_type=F32)
               for c, hd in items}
    kv = {(c, hd): jnp.dot(k_rem_t[c, hd], v_ref[rows(c), vcols(hd)], preferred_element_type=F32)
          for c, hd in items}

    for c in range(chunks):
        st = [st_ref[hd] for hd in range(GLA_HEADS)]
        o = [jnp.dot(q_dec[c, hd], st[hd].astype(BF16), preferred_element_type=F32) + o_intra[c, hd]
             for hd in range(GLA_HEADS)]
        for hd in range(GLA_HEADS):
            st_ref[hd] = st[hd] * jnp.concatenate([dec_t[c, hd]] * (GLA_DV // C), axis=1) + kv[c, hd]
            ms = jnp.mean(o[hd] * o[hd], axis=-1, keepdims=True)
            y = o[hd] * lax.rsqrt(ms + NORM_EPS) * g_ref[...]
            o_ref[rows(c), vcols(hd)] = (y * sg_ref[rows(c), vcols(hd)].astype(F32)).astype(o_ref.dtype)


def _gla(qkv, log_a, silu, out_g, B, S, tb=512):
    T = B * S
    nt = S // tb
    tok = lambda b, i: b * nt + i
    return pl.pallas_call(
        functools.partial(_gla_kernel, chunks=tb // GLA_CHUNK),
        grid=(B, nt),
        in_specs=[pl.BlockSpec((tb, GLA_WK), lambda b, i: (tok(b, i), 0)),
                  pl.BlockSpec((tb, GLA_WK), lambda b, i: (tok(b, i), 1)),
                  pl.BlockSpec((tb, GLA_WV), lambda b, i: (tok(b, i), 2 * GLA_WK // GLA_WV)),
                  pl.BlockSpec((tb, GLA_WK), lambda b, i: (tok(b, i), 0)),
                  pl.BlockSpec((tb, GLA_WV), lambda b, i: (tok(b, i), 0)),
                  pl.BlockSpec((1, GLA_DV), lambda b, i: (0, 0))],
        out_specs=pl.BlockSpec((tb, GLA_WV), lambda b, i: (tok(b, i), 0)),
        out_shape=jax.ShapeDtypeStruct((T, GLA_WV), BF16),
        scratch_shapes=[pltpu.VMEM((GLA_HEADS, GLA_DK, GLA_DV), F32)],
        compiler_params=_params("parallel", "arbitrary"),
        name="gla",
    )(qkv, qkv, qkv, log_a, silu, out_g.reshape(1, GLA_DV))


def _key_extras(S):
    nb = S // MOBA_BLOCK
    assert nb <= MOBA_ROUTE_ROWS and S <= LANES * LANES
    pos = jnp.arange(S, dtype=jnp.int32)
    p_hi = ((pos // LANES) * LANES).astype(F32)
    p_lo = (pos % LANES).astype(F32)
    onehot = (jnp.arange(MOBA_ROUTE_ROWS)[None, :] == (jnp.arange(S) // MOBA_BLOCK)[:, None]).astype(F32)
    one = jnp.ones((S,), F32)
    tail = jnp.stack([p_hi, p_hi, p_lo, p_lo, one, one], axis=1)
    pad = jnp.zeros((S, MOBA_DH - MOBA_ROUTE_ROWS - tail.shape[1]), F32)
    return jnp.concatenate([onehot, tail, pad], axis=1).astype(BF16)


def _route_kernel(qt_ref, k_ref, o_ref, km_ref, *, per_step):
    nb, dh, blk = qt_ref.shape
    rr = MOBA_ROUTE_ROWS
    km_ref[...] = jnp.zeros_like(km_ref)

    def block_mean(j, carry):
        k_j = k_ref[pl.ds(pl.multiple_of(j * blk, blk), blk), :]
        km_ref[pl.ds(j, 1), :] = jnp.mean(k_j[:, :dh].astype(F32), axis=0, keepdims=True)
        return carry

    lax.fori_loop(0, nb, block_mean, 0)
    km_hi, km_lo = _split_bf16(km_ref[...])
    km2 = jnp.concatenate([km_hi, km_lo], axis=1).astype(BF16)

    width = per_step * blk
    jidx = lax.broadcasted_iota(jnp.int32, (rr, width), 0)
    jf = jidx.astype(F32)
    sub = lax.broadcasted_iota(jnp.int32, (rr, width), 1) // blk

    def step(g, carry):
        q = jnp.concatenate([qt_ref[g * per_step + i] for i in range(per_step)], axis=1)
        gate = jnp.dot(km2, jnp.concatenate([q, q], axis=0), preferred_element_type=F32)
        past = jidx < g * per_step + sub
        gate = jnp.where(past, gate, -jnp.inf)
        sel = jnp.zeros((rr, width), F32)
        for _ in range(MOBA_TOPK):
            best = jnp.max(gate, axis=0, keepdims=True)
            first = jnp.min(jnp.where(gate == best, jf, float(rr)), axis=0, keepdims=True)
            pick = jf == first
            sel = jnp.where(pick, 1.0, sel)
            gate = jnp.where(pick, -jnp.inf, gate)
        route = jnp.where((sel > 0.0) & past, 0.0, NEG_BIG).astype(o_ref.dtype)
        for i in range(per_step):
            o_ref[g * per_step + i] = route[:, i * blk:(i + 1) * blk]
        return carry

    lax.fori_loop(0, nb // per_step, step, 0)


def _moba_route(q_t, k_hat, per_step=4):
    B, nb, W, blk = q_t.shape
    S = nb * blk
    per_step = min(per_step, nb)
    return pl.pallas_call(
        functools.partial(_route_kernel, per_step=per_step),
        grid=(B, MOBA_HEADS),
        in_specs=[pl.BlockSpec((None, nb, MOBA_DH, blk), lambda b, h: (b, 0, h, 0)),
                  pl.BlockSpec((None, S, MOBA_KHAT), lambda b, h: (b, 0, h))],
        out_specs=pl.BlockSpec((None, None, nb, MOBA_ROUTE_ROWS, blk), lambda b, h: (b, h, 0, 0, 0)),
        out_shape=jax.ShapeDtypeStruct((B, MOBA_HEADS, nb, MOBA_ROUTE_ROWS, blk), BF16),
        scratch_shapes=[pltpu.VMEM((MOBA_ROUTE_ROWS, MOBA_DH), F32)],
        compiler_params=_params("parallel", "parallel"),
        name="moba_route",
    )(q_t, k_hat)


def _moba_kernel(slope_ref, qt_ref, rt_ref, k_ref, vg_ref, vo_ref, sm_ref, o_ref,
                 m_ref, a_ref, acc_ref, s_own_ref,
                 s0_ref, s1_ref, s2_ref, s3_ref, p0_ref, p1_ref, *, group):
    hd = pl.program_id(1)
    qb = pl.program_id(2)
    blk = MOBA_BLOCK
    rr = MOBA_ROUTE_ROWS
    slope = slope_ref[hd]
    q_t = qt_ref[...]

    xr = MOBA_DH - rr
    ridx = lax.broadcasted_iota(jnp.int32, (xr, blk), 0)
    s_hi, s_lo = _split_bf16(jnp.full((xr, blk), slope, F32))
    c_hi, c_lo = _split_bf16(jnp.full((xr, blk), -slope * (qb * blk).astype(F32), F32))
    xrows = jnp.where((ridx == 0) | (ridx == 2), s_hi,
                      jnp.where((ridx == 1) | (ridx == 3), s_lo,
                                jnp.where(ridx == 4, c_hi, jnp.where(ridx == 5, c_lo, 0.0))))
    xrows = xrows.astype(BF16)
    q_hat = jnp.concatenate([q_t, rt_ref[...], xrows], axis=0)
    q_own = jnp.concatenate([q_t, jnp.zeros((rr, blk), BF16), xrows], axis=0)

    ones = jnp.ones((MOBA_SUM_ROWS, blk), BF16)

    def pv(v_u, p_ref):
        return jnp.dot(jnp.concatenate([v_u, ones], axis=0), p_ref[...], preferred_element_type=F32)

    unit = group * blk // MOBA_STEPS
    n_groups = (qb + group - 1) // group
    last_unit = jnp.maximum(n_groups * MOBA_STEPS - 1, 0)
    s_refs = (s0_ref, s1_ref, s2_ref, s3_ref)
    p_refs = (p0_ref, p1_ref)

    def scores(t):
        row0 = jnp.minimum(t, last_unit) * unit
        k_u = k_ref[pl.ds(pl.multiple_of(row0, unit), unit), :]
        return jnp.dot(k_u, q_hat, preferred_element_type=F32)

    k_own = k_ref[pl.ds(pl.multiple_of(qb * blk, blk), blk), :]
    s_own_ref[...] = jnp.dot(k_own, q_own, preferred_element_type=F32)
    for t in range(MOBA_AHEAD):
        s_refs[t][...] = scores(t)
    key = lax.broadcasted_iota(jnp.int32, (blk, blk), 0)
    qry = lax.broadcasted_iota(jnp.int32, (blk, blk), 1)
    s = jnp.where(key <= qry, s_own_ref[...], NEG_BIG)
    m0 = jnp.max(s, axis=0, keepdims=True)
    m_ref[...] = m0
    p0_ref[...] = jnp.exp((s - m0).astype(BF16))
    acc_ref[...] = pv(vo_ref[...], p0_ref)
    p1_ref[...] = jnp.zeros_like(p1_ref)
    a_ref[...] = jnp.ones_like(a_ref)

    def past_groups(g0, count):
        m, pending = m_ref[...], a_ref[...]
        for gi in [g0 + d for d in range(count)]:
            prev = jnp.maximum(gi - 1, 0)
            for k in range(MOBA_STEPS):
                s_refs[(k + MOBA_AHEAD) % MOBA_STEPS][...] = scores(gi * MOBA_STEPS + k + MOBA_AHEAD)
                s = s_refs[k][...]
                m_new = jnp.maximum(m, jnp.max(s, axis=0, keepdims=True))
                alpha = jnp.exp(m - m_new)
                p_refs[k % 2][...] = jnp.exp((s - m_new).astype(BF16))
                v_prev = (vg_ref[prev, :, (MOBA_STEPS - 1) * unit:MOBA_STEPS * unit] if k == 0
                          else vg_ref[gi, :, (k - 1) * unit:k * unit])
                acc_ref[...] = pending * acc_ref[...] + pv(v_prev, p_refs[(k - 1) % 2])
                m, pending = m_new, alpha
        m_ref[...], a_ref[...] = m, pending

    def two_groups(i, carry):
        past_groups(2 * i, 2)
        return carry

    lax.fori_loop(0, n_groups // 2, two_groups, 0)

    @pl.when(n_groups % 2 == 1)
    def _():
        past_groups(n_groups - 1, 1)

    last = jnp.maximum(n_groups - 1, 0)
    acc = a_ref[...] * acc_ref[...] + pv(vg_ref[last, :, (MOBA_STEPS - 1) * unit:MOBA_STEPS * unit], p1_ref)
    o_t = acc[:MOBA_DH] / acc[MOBA_DH:MOBA_DH + 1]
    o_ref[...] = (o_t.T * sm_ref[...].astype(F32)).astype(o_ref.dtype)


def _moba(q_t, route, k_hat, v_grp, v_own, silu, silu_col0, slopes):
    B, nb, W, blk = q_t.shape
    S = nb * blk
    dh = MOBA_DH
    c0 = silu_col0 // dh
    ng, grows = v_grp.shape[1], v_grp.shape[3]
    unit = grows // MOBA_STEPS
    return pl.pallas_call(
        functools.partial(_moba_kernel, group=grows // blk),
        grid=(B, MOBA_HEADS, nb),
        in_specs=[pl.BlockSpec(memory_space=pltpu.SMEM),
                  pl.BlockSpec((None, None, dh, blk), lambda b, h, i: (b, i, h, 0)),
                  pl.BlockSpec((None, None, None, MOBA_ROUTE_ROWS, blk), lambda b, h, i: (b, h, i, 0, 0)),
                  pl.BlockSpec((None, S, MOBA_KHAT), lambda b, h, i: (b, 0, h)),
                  pl.BlockSpec((None, ng, dh, grows), lambda b, h, i: (b, 0, h, 0)),
                  pl.BlockSpec((None, None, dh, blk), lambda b, h, i: (b, i, h, 0)),
                  pl.BlockSpec((None, blk, dh), lambda b, h, i: (b, i, c0 + h))],
        out_specs=pl.BlockSpec((None, blk, dh), lambda b, h, i: (b, i, h)),
        out_shape=jax.ShapeDtypeStruct((B, S, W), BF16),
        scratch_shapes=[pltpu.VMEM((1, blk), F32),
                        pltpu.VMEM((1, blk), F32),
                        pltpu.VMEM((dh + MOBA_SUM_ROWS, blk), F32),
                        pltpu.VMEM((blk, blk), F32)]
                       + [pltpu.VMEM((unit, blk), F32)] * MOBA_STEPS
                       + [pltpu.VMEM((unit, blk), BF16)] * 2,
        compiler_params=_params("parallel", "parallel", "parallel"),
        name="moba",
    )(slopes, q_t, route, k_hat, v_grp, v_own, silu)


def _tail_kernel(ya_ref, yb_ref, ga_ref, gb_ref, x_ref, wa_ref, wb_ref, wo_ref, o_ref):
    za = jnp.dot(ya_ref[...], wa_ref[...], preferred_element_type=F32)
    zb = jnp.dot(yb_ref[...], wb_ref[...], preferred_element_type=F32)
    merged = ga_ref[...].astype(F32) * za + gb_ref[...].astype(F32) * zb
    o_ref[...] = x_ref[...] + jnp.dot(merged.astype(BF16), wo_ref[...], preferred_element_type=F32)


def _tail(ya, yb, gates, x, wa, wb, wo, tm=256):
    T, D = x.shape
    const = lambda t: (0, 0)
    return pl.pallas_call(
        _tail_kernel,
        grid=(T // tm,),
        in_specs=[pl.BlockSpec((tm, GLA_WV), lambda t: (t, 0)),
                  pl.BlockSpec((tm, MOBA_W), lambda t: (t, 0)),
                  pl.BlockSpec((tm, D), lambda t: (t, 0)),
                  pl.BlockSpec((tm, D), lambda t: (t, 1)),
                  pl.BlockSpec((tm, D), lambda t: (t, 0)),
                  pl.BlockSpec((GLA_WV, D), const),
                  pl.BlockSpec((MOBA_W, D), const),
                  pl.BlockSpec((D, D), const)],
        out_specs=pl.BlockSpec((tm, D), lambda t: (t, 0)),
        out_shape=jax.ShapeDtypeStruct((T, D), F32),
        compiler_params=_params("parallel"),
        name="merge_out",
    )(ya, yb, gates, gates, x, wa, wb, wo)


def kernel(x, norm_g, w_in, w_gla_gate, b_gla_gate, gla_out_g, q_norm_g, k_norm_g,
           w_branch_gla, w_branch_moba, w_out):
    B, S, D = x.shape
    T = B * S
    sizes = (GLA_WK, GLA_WK, GLA_WV, GLA_GATE_RANK, GLA_WV, MOBA_W, MOBA_W, MOBA_W, MOBA_W, D, D)
    offs = [0]
    for n in sizes:
        offs.append(offs[-1] + n)
    slopes = jnp.exp2(-8.0 * jnp.arange(1, MOBA_HEADS + 1, dtype=F32) / MOBA_HEADS)

    for layer in range(norm_g.shape[0]):
        w = w_in[layer]
        cols = lambda i, j=None: w[:, offs[i]:offs[(i if j is None else j) + 1]]
        w_qkv = cols(0, 2).astype(BF16)
        w_lr = cols(3).astype(BF16)
        w_silu = jnp.concatenate([cols(4), cols(8)], axis=1).astype(BF16)
        w_mq_t = cols(5).T.astype(BF16)
        w_mk = cols(6).astype(BF16)
        w_mv_t = cols(7).T.astype(BF16)
        w_gates = cols(9, 10).astype(BF16)

        h, h_t = _rms_norm_both(x, norm_g[layer])
        h = h.reshape(T, D)

        qkv = _proj(h, w_qkv, lambda a: a)
        log_a = _log_decay(h, w_lr, w_gla_gate[layer], b_gla_gate[layer])
        silu = _proj(h, w_silu, _silu)
        gates = _proj(h, w_gates, jax.nn.sigmoid)
        k_hat = _proj_khat(h, w_mk, k_norm_g[layer], _key_extras(S))
        (m_q_t,) = _proj_t(w_mq_t, h_t, q_norm_g[layer], True, (MOBA_BLOCK,), scale=MOBA_DH ** -0.5)
        v_own, v_grp = _proj_t(w_mv_t, h_t, q_norm_g[layer], False,
                               (MOBA_BLOCK, min(MOBA_GROUP * MOBA_BLOCK, S)))

        ya = _gla(qkv, log_a, silu, gla_out_g[layer], B, S)
        k_hat = k_hat.reshape(B, S, MOBA_HEADS * MOBA_KHAT)
        yb = _moba(m_q_t, _moba_route(m_q_t, k_hat), k_hat, v_grp, v_own,
                   silu.reshape(B, S, 2 * GLA_WV), GLA_WV, slopes)

        x = _tail(ya, yb.reshape(T, MOBA_W), gates, x.reshape(T, D),
                  w_branch_gla[layer].astype(BF16), w_branch_moba[layer].astype(BF16),
                  w_out[layer].astype(BF16)).reshape(B, S, D)
    return x
```

```python
import functools

import jax
import jax.numpy as jnp
from jax import lax
from jax.experimental import pallas as pl
from jax.experimental.pallas import tpu as pltpu

F32 = jnp.float32
BF16 = jnp.bfloat16

GLA_HEADS = 4
GLA_DK = 128
GLA_DV = 256
GLA_WK = GLA_HEADS * GLA_DK
GLA_WV = GLA_HEADS * GLA_DV
GLA_GATE_RANK = 16
GLA_TAU = 16.0
GLA_CHUNK = 128

MOBA_HEADS = 8
MOBA_DH = 128
MOBA_W = MOBA_HEADS * MOBA_DH
MOBA_BLOCK = 256
MOBA_TOPK = 3
MOBA_KHAT = 2 * MOBA_DH
MOBA_ROUTE_ROWS = 64
MOBA_GROUP = 4
MOBA_SUM_ROWS = 16
MOBA_STEPS = 4
MOBA_AHEAD = 3

NORM_EPS = 1e-6
NEG_BIG = -1e30
LANES = 128
VMEM_LIMIT_BYTES = 48 * 1024 * 1024


def _params(*semantics):
    return pltpu.CompilerParams(dimension_semantics=semantics,
                                vmem_limit_bytes=VMEM_LIMIT_BYTES)


def _split_bf16(v):
    hi = lax.bitcast_convert_type(lax.bitcast_convert_type(v, jnp.uint32) & jnp.uint32(0xFFFF0000), F32)
    return hi, v - hi


def _norm_kernel(x_ref, g_ref, h_ref, ht_ref):
    x = x_ref[...]
    ms = jnp.mean(x * x, axis=-1, keepdims=True)
    y = x * lax.rsqrt(ms + NORM_EPS) * g_ref[...]
    h_ref[...] = y.astype(BF16)
    ht_ref[...] = y.T.astype(BF16)


def _rms_norm_both(x, g, tm=512):
    B, S, D = x.shape
    return pl.pallas_call(
        _norm_kernel,
        grid=(B, S // tm),
        in_specs=[pl.BlockSpec((None, tm, D), lambda b, s: (b, s, 0)),
                  pl.BlockSpec((1, D), lambda b, s: (0, 0))],
        out_specs=[pl.BlockSpec((None, tm, D), lambda b, s: (b, s, 0)),
                   pl.BlockSpec((None, D, tm), lambda b, s: (b, 0, s))],
        out_shape=[jax.ShapeDtypeStruct((B, S, D), BF16),
                   jax.ShapeDtypeStruct((B, D, S), BF16)],
        compiler_params=_params("parallel", "parallel"),
        name="rms_norm",
    )(x, g.reshape(1, D))


def _silu(a):
    return a * jax.nn.sigmoid(a)


def _proj_kernel(h_ref, w_ref, o_ref, *, epilogue):
    acc = jnp.dot(h_ref[...], w_ref[...], preferred_element_type=F32)
    o_ref[...] = epilogue(acc).astype(o_ref.dtype)


def _proj(h, w, epilogue, tm=1024, tn=2048):
    T, D = h.shape
    N = w.shape[1]
    tn = min(tn, N)
    return pl.pallas_call(
        functools.partial(_proj_kernel, epilogue=epilogue),
        grid=(N // tn, T // tm),
        in_specs=[pl.BlockSpec((tm, D), lambda n, t: (t, 0)),
                  pl.BlockSpec((D, tn), lambda n, t: (0, n), pipeline_mode=pl.Buffered(1))],
        out_specs=pl.BlockSpec((tm, tn), lambda n, t: (t, n)),
        out_shape=jax.ShapeDtypeStruct((T, N), BF16),
        compiler_params=_params("parallel", "parallel"),
        name="proj",
    )(h, w)


def _proj_khat_kernel(h_ref, w_ref, g_ref, e_ref, o_ref):
    acc = jnp.dot(h_ref[...], w_ref[...], preferred_element_type=F32)
    g = g_ref[...]
    for hd in range(acc.shape[1] // MOBA_DH):
        a = acc[:, hd * MOBA_DH:(hd + 1) * MOBA_DH]
        ms = jnp.mean(a * a, axis=-1, keepdims=True)
        base = hd * MOBA_KHAT
        o_ref[:, base:base + MOBA_DH] = (a * lax.rsqrt(ms + NORM_EPS) * g).astype(o_ref.dtype)
        o_ref[:, base + MOBA_DH:base + MOBA_KHAT] = e_ref[...]


def _proj_khat(h, w, g, extras, tm=512):
    T, D = h.shape
    N = w.shape[1]
    ns = extras.shape[0] // tm
    return pl.pallas_call(
        _proj_khat_kernel,
        grid=(T // tm,),
        in_specs=[pl.BlockSpec((tm, D), lambda t: (t, 0)),
                  pl.BlockSpec((D, N), lambda t: (0, 0)),
                  pl.BlockSpec((1, MOBA_DH), lambda t: (0, 0)),
                  pl.BlockSpec((tm, MOBA_DH), lambda t: (t % ns, 0))],
        out_specs=pl.BlockSpec((tm, MOBA_HEADS * MOBA_KHAT), lambda t: (t, 0)),
        out_shape=jax.ShapeDtypeStruct((T, MOBA_HEADS * MOBA_KHAT), BF16),
        compiler_params=_params("parallel"),
        name="proj_khat",
    )(h, w, g.reshape(1, MOBA_DH), extras)


def _projt_kernel(wt_ref, ht_ref, g_ref, *o_refs, norm, scale):
    acc = jnp.dot(wt_ref[...], ht_ref[...], preferred_element_type=F32)
    tn, tm = acc.shape
    if norm:
        a = acc.reshape(tn // MOBA_DH, MOBA_DH, tm)
        ms = jnp.mean(a * a, axis=1, keepdims=True)
        a = a * lax.rsqrt(ms + NORM_EPS) * (g_ref[...] * scale)[None]
        acc = a.reshape(tn, tm)
    for o_ref in o_refs:
        blk = o_ref.shape[-1]
        for i in range(tm // blk):
            o_ref[i] = acc[:, i * blk:(i + 1) * blk].astype(o_ref.dtype)


def _proj_t(wt, ht, g, norm, blks, scale=1.0, tn=512, tm=1024):
    N, D = wt.shape
    B, _, S = ht.shape
    tm = min(max(tm, *blks), S)
    return pl.pallas_call(
        functools.partial(_projt_kernel, norm=norm, scale=scale),
        grid=(B, S // tm, N // tn),
        in_specs=[pl.BlockSpec((tn, D), lambda b, s, n: (n, 0)),
                  pl.BlockSpec((None, D, tm), lambda b, s, n: (b, 0, s)),
                  pl.BlockSpec((MOBA_DH, 1), lambda b, s, n: (0, 0))],
        out_specs=[pl.BlockSpec((None, tm // blk, tn, blk), lambda b, s, n: (b, s, n, 0)) for blk in blks],
        out_shape=[jax.ShapeDtypeStruct((B, S // blk, N, blk), BF16) for blk in blks],
        compiler_params=_params("parallel", "parallel", "parallel"),
        name="proj_t",
    )(wt, ht, g.reshape(MOBA_DH, 1))


def _loga_kernel(h_ref, wlr_ref, wg_ref, b_ref, o_ref):
    lr = jnp.dot(h_ref[...], wlr_ref[...], preferred_element_type=F32)
    lr_hi, lr_lo = _split_bf16(lr)
    lane = lax.broadcasted_iota(jnp.int32, lr.shape, 1)
    lr_pack = jnp.where(lane < 2 * GLA_GATE_RANK, lr_hi, lr_lo).astype(BF16)
    z = jnp.dot(lr_pack, wg_ref[...], preferred_element_type=F32) + b_ref[...]
    log_sig = jnp.minimum(z, 0.0) - jnp.log1p(jnp.exp(-jnp.abs(z)))
    o_ref[...] = log_sig * (1.0 / GLA_TAU)


def _log_decay(h, w_lr, w_gate, b_gate, tm=512):
    T, D = h.shape
    w_lr = jnp.pad(jnp.concatenate([w_lr] * 3, axis=1), ((0, 0), (0, LANES - 3 * GLA_GATE_RANK)))
    g_hi, g_lo = _split_bf16(w_gate)
    w_gate = jnp.pad(jnp.concatenate([g_hi, g_lo, g_hi], axis=0),
                     ((0, LANES - 3 * GLA_GATE_RANK), (0, 0))).astype(BF16)
    return pl.pallas_call(
        _loga_kernel,
        grid=(T // tm,),
        in_specs=[pl.BlockSpec((tm, D), lambda t: (t, 0)),
                  pl.BlockSpec((D, LANES), lambda t: (0, 0)),
                  pl.BlockSpec((LANES, GLA_WK), lambda t: (0, 0)),
                  pl.BlockSpec((1, GLA_WK), lambda t: (0, 0))],
        out_specs=pl.BlockSpec((tm, GLA_WK), lambda t: (t, 0)),
        out_shape=jax.ShapeDtypeStruct((T, GLA_WK), F32),
        compiler_params=_params("parallel"),
        name="log_decay",
    )(h, w_lr, w_gate, b_gate.reshape(1, GLA_WK))


def _gla_kernel(q_ref, k_ref, v_ref, la_ref, sg_ref, g_ref, o_ref, st_ref, *, chunks):
    @pl.when(pl.program_id(1) == 0)
    def _():
        st_ref[...] = jnp.zeros_like(st_ref)

    C = GLA_CHUNK
    row = lax.broadcasted_iota(jnp.int32, (C, C), 0)
    col = lax.broadcasted_iota(jnp.int32, (C, C), 1)
    causal = col <= row
    tril3 = jnp.concatenate([causal.astype(BF16)] * 3, axis=1)
    scale = GLA_DK ** -0.5

    def cumsum(x):
        hi, rest = _split_bf16(x)
        mid, lo = _split_bf16(rest)
        parts = jnp.concatenate([hi, mid, lo], axis=0).astype(BF16)
        return jnp.dot(tril3, parts, preferred_element_type=F32)

    items = [(c, hd) for c in range(chunks) for hd in range(GLA_HEADS)]
    rows = lambda c: slice(c * C, (c + 1) * C)
    kcols = lambda hd: slice(hd * GLA_DK, (hd + 1) * GLA_DK)
    vcols = lambda hd: slice(hd * GLA_DV, (hd + 1) * GLA_DV)

    b = {(c, hd): cumsum(la_ref[rows(c), kcols(hd)]) for c, hd in items}
    q_dec, k_inv_t, k_rem_t, dec_t = {}, {}, {}, {}
    for c, hd in items:
        bi = b[c, hd]
        b_last = bi[C - 1:C, :]
        q = q_ref[rows(c), kcols(hd)].astype(F32) * scale
        k = k_ref[rows(c), kcols(hd)].astype(F32)
        q_dec[c, hd] = (q * jnp.exp(bi)).astype(BF16)
        k_inv_t[c, hd] = (k * jnp.exp(-bi)).T.astype(BF16)
        k_rem_t[c, hd] = (k * jnp.exp(b_last - bi)).T.astype(BF16)
        dec_t[c, hd] = jnp.broadcast_to(jnp.exp(b_last), (C, GLA_DK)).T
    attn = {it: jnp.dot(q_dec[it], k_inv_t[it], preferred_element_type=F32) for it in items}
    attn = {it: jnp.where(causal, attn[it], 0.0).astype(BF16) for it in items}
    o_intra = {(c, hd): jnp.dot(attn[c, hd], v_ref[rows(c), vcols(hd)], preferred_element_type=F32)
               for c, hd in items}
    kv = {(c, hd): jnp.dot(k_rem_t[c, hd], v_ref[rows(c), vcols(hd)], preferred_element_type=F32)
          for c, hd in items}

    for c in range(chunks):
        st = [st_ref[hd] for hd in range(GLA_HEADS)]
        o = [jnp.dot(q_dec[c, hd], st[hd].astype(BF16), preferred_element_type=F32) + o_intra[c, hd]
             for hd in range(GLA_HEADS)]
        for hd in range(GLA_HEADS):
            st_ref[hd] = st[hd] * jnp.concatenate([dec_t[c, hd]] * (GLA_DV // C), axis=1) + kv[c, hd]
            ms = jnp.mean(o[hd] * o[hd], axis=-1, keepdims=True)
            y = o[hd] * lax.rsqrt(ms + NORM_EPS) * g_ref[...]
            o_ref[rows(c), vcols(hd)] = (y * sg_ref[rows(c), vcols(hd)].astype(F32)).astype(o_ref.dtype)


def _gla(qkv, log_a, silu, out_g, B, S, tb=512):
    T = B * S
    nt = S // tb
    tok = lambda b, i: b * nt + i
    return pl.pallas_call(
        functools.partial(_gla_kernel, chunks=tb // GLA_CHUNK),
        grid=(B, nt),
        in_specs=[pl.BlockSpec((tb, GLA_WK), lambda b, i: (tok(b, i), 0)),
                  pl.BlockSpec((tb, GLA_WK), lambda b, i: (tok(b, i), 1)),
                  pl.BlockSpec((tb, GLA_WV), lambda b, i: (tok(b, i), 2 * GLA_WK // GLA_WV)),
                  pl.BlockSpec((tb, GLA_WK), lambda b, i: (tok(b, i), 0)),
                  pl.BlockSpec((tb, GLA_WV), lambda b, i: (tok(b, i), 0)),
                  pl.BlockSpec((1, GLA_DV), lambda b, i: (0, 0))],
        out_specs=pl.BlockSpec((tb, GLA_WV), lambda b, i: (tok(b, i), 0)),
        out_shape=jax.ShapeDtypeStruct((T, GLA_WV), BF16),
        scratch_shapes=[pltpu.VMEM((GLA_HEADS, GLA_DK, GLA_DV), F32)],
        compiler_params=_params("parallel", "arbitrary"),
        name="gla",
    )(qkv, qkv, qkv, log_a, silu, out_g.reshape(1, GLA_DV))


def _key_extras(S):
    nb = S // MOBA_BLOCK
    assert nb <= MOBA_ROUTE_ROWS and S <= LANES * LANES
    pos = jnp.arange(S, dtype=jnp.int32)
    p_hi = ((pos // LANES) * LANES).astype(F32)
    p_lo = (pos % LANES).astype(F32)
    onehot = (jnp.arange(MOBA_ROUTE_ROWS)[None, :] == (jnp.arange(S) // MOBA_BLOCK)[:, None]).astype(F32)
    one = jnp.ones((S,), F32)
    tail = jnp.stack([p_hi, p_hi, p_lo, p_lo, one, one], axis=1)
    pad = jnp.zeros((S, MOBA_DH - MOBA_ROUTE_ROWS - tail.shape[1]), F32)
    return jnp.concatenate([onehot, tail, pad], axis=1).astype(BF16)


def _moba_kernel(slope_ref, qt_ref, k_ref, vg_ref, vo_ref, sm_ref, o_ref,
                 kmean_ref, m_ref, a_ref, acc_ref,
                 s0_ref, s1_ref, s2_ref, s3_ref, p0_ref, p1_ref, *, group):
    hd = pl.program_id(1)
    qb = pl.program_id(2)
    blk = MOBA_BLOCK
    rr = MOBA_ROUTE_ROWS
    slope = slope_ref[hd]

    @pl.when(qb == 0)
    def _():
        kmean_ref[...] = jnp.zeros_like(kmean_ref)

    q_t = qt_ref[...]

    jidx = lax.broadcasted_iota(jnp.int32, (rr, blk), 0)
    jf = jidx.astype(F32)
    past = jidx < qb
    km_hi, km_lo = _split_bf16(kmean_ref[...])
    gate = jnp.dot(jnp.concatenate([km_hi, km_lo], axis=1).astype(BF16),
                   jnp.concatenate([q_t, q_t], axis=0), preferred_element_type=F32)
    gate = jnp.where(past, gate, -jnp.inf)
    sel = jnp.zeros((rr, blk), F32)
    for _ in range(MOBA_TOPK):
        best = jnp.max(gate, axis=0, keepdims=True)
        first = jnp.min(jnp.where(gate == best, jf, float(rr)), axis=0, keepdims=True)
        pick = jf == first
        sel = jnp.where(pick, 1.0, sel)
        gate = jnp.where(pick, -jnp.inf, gate)
    route = jnp.where((sel > 0.0) & past, 0.0, NEG_BIG)

    xr = MOBA_DH - rr
    ridx = lax.broadcasted_iota(jnp.int32, (xr, blk), 0)
    s_hi, s_lo = _split_bf16(jnp.full((xr, blk), slope, F32))
    c_hi, c_lo = _split_bf16(jnp.full((xr, blk), -slope * (qb * blk).astype(F32), F32))
    xrows = jnp.where((ridx == 0) | (ridx == 2), s_hi,
                      jnp.where((ridx == 1) | (ridx == 3), s_lo,
                                jnp.where(ridx == 4, c_hi, jnp.where(ridx == 5, c_lo, 0.0))))
    xrows = xrows.astype(BF16)
    q_hat = jnp.concatenate([q_t, route.astype(BF16), xrows], axis=0)
    q_own = jnp.concatenate([q_t, jnp.zeros((rr, blk), BF16), xrows], axis=0)

    key = lax.broadcasted_iota(jnp.int32, (blk, blk), 0)
    qry = lax.broadcasted_iota(jnp.int32, (blk, blk), 1)
    k_own = k_ref[pl.ds(pl.multiple_of(qb * blk, blk), blk), :]
    kmean_ref[pl.ds(qb, 1), :] = jnp.mean(k_own[:, :MOBA_DH].astype(F32), axis=0, keepdims=True)
    s = jnp.dot(k_own, q_own, preferred_element_type=F32)
    s = jnp.where(key <= qry, s, NEG_BIG)
    m0 = jnp.max(s, axis=0, keepdims=True)
    m_ref[...] = m0
    p0_ref[...] = jnp.exp((s - m0).astype(BF16))

    ones = jnp.ones((MOBA_SUM_ROWS, blk), BF16)

    def pv(v_u, p_ref):
        return jnp.dot(jnp.concatenate([v_u, ones], axis=0), p_ref[...], preferred_element_type=F32)

    acc_ref[...] = pv(vo_ref[...], p0_ref)

    unit = group * blk // MOBA_STEPS
    n_groups = (qb + group - 1) // group
    last_unit = jnp.maximum(n_groups * MOBA_STEPS - 1, 0)
    s_refs = (s0_ref, s1_ref, s2_ref, s3_ref)
    p_refs = (p0_ref, p1_ref)

    def scores(t):
        row0 = jnp.minimum(t, last_unit) * unit
        k_u = k_ref[pl.ds(pl.multiple_of(row0, unit), unit), :]
        return jnp.dot(k_u, q_hat, preferred_element_type=F32)

    for t in range(MOBA_AHEAD):
        s_refs[t][...] = scores(t)
    p1_ref[...] = jnp.zeros_like(p1_ref)
    a_ref[...] = jnp.ones_like(a_ref)

    def past_groups(g0, count):
        m, pending = m_ref[...], a_ref[...]
        for gi in [g0 + d for d in range(count)]:
            prev = jnp.maximum(gi - 1, 0)
            for k in range(MOBA_STEPS):
                s_refs[(k + MOBA_AHEAD) % MOBA_STEPS][...] = scores(gi * MOBA_STEPS + k + MOBA_AHEAD)
                s = s_refs[k][...]
                m_new = jnp.maximum(m, jnp.max(s, axis=0, keepdims=True))
                alpha = jnp.exp(m - m_new)
                p_refs[k % 2][...] = jnp.exp((s - m_new).astype(BF16))
                v_prev = (vg_ref[prev, :, (MOBA_STEPS - 1) * unit:MOBA_STEPS * unit] if k == 0
                          else vg_ref[gi, :, (k - 1) * unit:k * unit])
                acc_ref[...] = pending * acc_ref[...] + pv(v_prev, p_refs[(k - 1) % 2])
                m, pending = m_new, alpha
        m_ref[...], a_ref[...] = m, pending

    def two_groups(i, carry):
        past_groups(2 * i, 2)
        return carry

    lax.fori_loop(0, n_groups // 2, two_groups, 0)

    @pl.when(n_groups % 2 == 1)
    def _():
        past_groups(n_groups - 1, 1)

    last = jnp.maximum(n_groups - 1, 0)
    acc = a_ref[...] * acc_ref[...] + pv(vg_ref[last, :, (MOBA_STEPS - 1) * unit:MOBA_STEPS * unit], p1_ref)
    o_t = acc[:MOBA_DH] / acc[MOBA_DH:MOBA_DH + 1]
    o_ref[...] = (o_t.T * sm_ref[...].astype(F32)).astype(o_ref.dtype)


def _moba(q_t, k_hat, v_grp, v_own, silu, silu_col0, slopes):
    B, nb, W, blk = q_t.shape
    S = nb * blk
    dh = MOBA_DH
    c0 = silu_col0 // dh
    ng, grows = v_grp.shape[1], v_grp.shape[3]
    unit = grows // MOBA_STEPS
    return pl.pallas_call(
        functools.partial(_moba_kernel, group=grows // blk),
        grid=(B, MOBA_HEADS, nb),
        in_specs=[pl.BlockSpec(memory_space=pltpu.SMEM),
                  pl.BlockSpec((None, None, dh, blk), lambda b, h, i: (b, i, h, 0)),
                  pl.BlockSpec((None, S, MOBA_KHAT), lambda b, h, i: (b, 0, h)),
                  pl.BlockSpec((None, ng, dh, grows), lambda b, h, i: (b, 0, h, 0)),
                  pl.BlockSpec((None, None, dh, blk), lambda b, h, i: (b, i, h, 0)),
                  pl.BlockSpec((None, blk, dh), lambda b, h, i: (b, i, c0 + h))],
        out_specs=pl.BlockSpec((None, blk, dh), lambda b, h, i: (b, i, h)),
        out_shape=jax.ShapeDtypeStruct((B, S, W), BF16),
        scratch_shapes=[pltpu.VMEM((MOBA_ROUTE_ROWS, dh), F32),
                        pltpu.VMEM((1, blk), F32),
                        pltpu.VMEM((1, blk), F32),
                        pltpu.VMEM((dh + MOBA_SUM_ROWS, blk), F32)]
                       + [pltpu.VMEM((unit, blk), F32)] * MOBA_STEPS
                       + [pltpu.VMEM((unit, blk), BF16)] * 2,
        compiler_params=_params("parallel", "parallel", "arbitrary"),
        name="moba",
    )(slopes, q_t, k_hat, v_grp, v_own, silu)


def _tail_kernel(ya_ref, yb_ref, ga_ref, gb_ref, x_ref, wa_ref, wb_ref, wo_ref, o_ref):
    za = jnp.dot(ya_ref[...], wa_ref[...], preferred_element_type=F32)
    zb = jnp.dot(yb_ref[...], wb_ref[...], preferred_element_type=F32)
    merged = ga_ref[...].astype(F32) * za + gb_ref[...].astype(F32) * zb
    o_ref[...] = x_ref[...] + jnp.dot(merged.astype(BF16), wo_ref[...], preferred_element_type=F32)


def _tail(ya, yb, gates, x, wa, wb, wo, tm=512):
    T, D = x.shape
    const = lambda t: (0, 0)
    return pl.pallas_call(
        _tail_kernel,
        grid=(T // tm,),
        in_specs=[pl.BlockSpec((tm, GLA_WV), lambda t: (t, 0)),
                  pl.BlockSpec((tm, MOBA_W), lambda t: (t, 0)),
                  pl.BlockSpec((tm, D), lambda t: (t, 0)),
                  pl.BlockSpec((tm, D), lambda t: (t, 1)),
                  pl.BlockSpec((tm, D), lambda t: (t, 0)),
                  pl.BlockSpec((GLA_WV, D), const, pipeline_mode=pl.Buffered(1)),
                  pl.BlockSpec((MOBA_W, D), const, pipeline_mode=pl.Buffered(1)),
                  pl.BlockSpec((D, D), const, pipeline_mode=pl.Buffered(1))],
        out_specs=pl.BlockSpec((tm, D), lambda t: (t, 0)),
        out_shape=jax.ShapeDtypeStruct((T, D), F32),
        compiler_params=_params("parallel"),
        name="merge_out",
    )(ya, yb, gates, gates, x, wa, wb, wo)


def kernel(x, norm_g, w_in, w_gla_gate, b_gla_gate, gla_out_g, q_norm_g, k_norm_g,
           w_branch_gla, w_branch_moba, w_out):
    B, S, D = x.shape
    T = B * S
    sizes = (GLA_WK, GLA_WK, GLA_WV, GLA_GATE_RANK, GLA_WV, MOBA_W, MOBA_W, MOBA_W, MOBA_W, D, D)
    offs = [0]
    for n in sizes:
        offs.append(offs[-1] + n)
    slopes = jnp.exp2(-8.0 * jnp.arange(1, MOBA_HEADS + 1, dtype=F32) / MOBA_HEADS)

    for layer in range(norm_g.shape[0]):
        w = w_in[layer]
        cols = lambda i, j=None: w[:, offs[i]:offs[(i if j is None else j) + 1]]
        w_qkv = cols(0, 2).astype(BF16)
        w_lr = cols(3).astype(BF16)
        w_silu = jnp.concatenate([cols(4), cols(8)], axis=1).astype(BF16)
        w_mq_t = cols(5).T.astype(BF16)
        w_mk = cols(6).astype(BF16)
        w_mv_t = cols(7).T.astype(BF16)
        w_gates = cols(9, 10).astype(BF16)

        h, h_t = _rms_norm_both(x, norm_g[layer])
        h = h.reshape(T, D)

        qkv = _proj(h, w_qkv, lambda a: a)
        log_a = _log_decay(h, w_lr, w_gla_gate[layer], b_gla_gate[layer])
        silu = _proj(h, w_silu, _silu)
        gates = _proj(h, w_gates, jax.nn.sigmoid)
        k_hat = _proj_khat(h, w_mk, k_norm_g[layer], _key_extras(S))
        (m_q_t,) = _proj_t(w_mq_t, h_t, q_norm_g[layer], True, (MOBA_BLOCK,), scale=MOBA_DH ** -0.5)
        v_own, v_grp = _proj_t(w_mv_t, h_t, q_norm_g[layer], False,
                               (MOBA_BLOCK, min(MOBA_GROUP * MOBA_BLOCK, S)))

        ya = _gla(qkv, log_a, silu, gla_out_g[layer], B, S)
        yb = _moba(m_q_t, k_hat.reshape(B, S, MOBA_HEADS * MOBA_KHAT), v_grp, v_own,
                   silu.reshape(B, S, 2 * GLA_WV), GLA_WV, slopes)

        x = _tail(ya, yb.reshape(T, MOBA_W), gates, x.reshape(T, D),
                  w_branch_gla[layer].astype(BF16), w_branch_moba[layer].astype(BF16),
                  w_out[layer].astype(BF16)).reshape(B, S, D)
    return x
```
